```python
import math
import jax, jax.numpy as jnp
from jax import lax
import numpy as np

D_MODEL = 2048
BATCH = 1
SEQ = 8192
DEPTH = 1

GRID_W = 64
NORM_EPS = 1e-6
A_HEADS = 4
A_HEAD_DIM = D_MODEL // 8
A_WIDTH = A_HEADS * A_HEAD_DIM
A_CHUNK = 128
CONV_W = 5
B_HEADS = 8
B_KV_HEADS = 2
B_HEAD_DIM = 128
B_WIDTH = B_HEADS * B_HEAD_DIM
B_KV_WIDTH = B_KV_HEADS * B_HEAD_DIM
Q_BLOCK = 128
ROPE_THETA = 10000.0
N_EXPERTS = 32
TOP_K = 4
D_FF = D_MODEL
SWIGLU_LIMIT = 7.0
SWIGLU_ALPHA = 1.702
MOE_BLOCK = 128
SPLITS = (A_WIDTH, A_WIDTH, A_WIDTH, A_WIDTH, 4 * A_HEADS, B_WIDTH, B_KV_WIDTH, B_KV_WIDTH, 2 * D_MODEL)
D_IN = sum(SPLITS)

kernel_name = "hybrid_mlstm_axialgqa_moe_encoder"


def rmsnorm(x, g):
    xf = x.astype(jnp.float32)
    y = xf * lax.rsqrt(jnp.mean(xf * xf, axis=-1, keepdims=True) + NORM_EPS)
    return (y * g.astype(jnp.float32)).astype(x.dtype)


def centred_depthwise_conv(x, w, b):
    y = lax.conv_general_dilated(x, w[:, None, :].astype(x.dtype), window_strides=(1,),
                                 padding=[(CONV_W // 2, CONV_W // 2)],
                                 dimension_numbers=('NWC', 'WIO', 'NWC'),
                                 feature_group_count=x.shape[-1])
    return y + b.astype(x.dtype)


def mlstm_direction(q, k, v, i_pre, f_pre):
    B, H, S, dh = q.shape
    nc = S // A_CHUNK
    q = q * (dh ** -0.5)
    log_f = jax.nn.log_sigmoid(f_pre)

    def chunks(t):
        return jnp.moveaxis(t.reshape(B, H, nc, A_CHUNK, *t.shape[3:]), 2, 0)

    causal = jnp.tril(jnp.ones((A_CHUNK, A_CHUNK), dtype=bool))

    def step(carry, xs):
        C, n, m = carry
        qc, kc, vc, ic, fc = xs
        b = jnp.cumsum(fc, axis=-1)
        d = b[..., :, None] - b[..., None, :] + ic[..., None, :]
        d = jnp.where(causal, d, -jnp.inf)
        m_inter = b + m[..., None]
        m_t = jnp.maximum(m_inter, jnp.max(d, axis=-1))
        w = jnp.exp(d - m_t[..., None]) * jnp.einsum('bhld,bhsd->bhls', qc, kc)
        s_inter = jnp.exp(m_inter - m_t)
        num = jnp.einsum('bhls,bhsd->bhld', w, vc) + s_inter[..., None] * jnp.einsum('bhld,bhde->bhle', qc, C)
        den = jnp.sum(w, axis=-1) + s_inter * jnp.einsum('bhld,bhd->bhl', qc, n)
        h = num / jnp.maximum(jnp.abs(den), jnp.exp(-m_t))[..., None]
        b_last = b[..., -1]
        g = b_last[..., None] - b + ic
        m_new = jnp.maximum(b_last + m, jnp.max(g, axis=-1))
        wk = jnp.exp(g - m_new[..., None])
        decay = jnp.exp(b_last + m - m_new)
        C = decay[..., None, None] * C + jnp.einsum('bhs,bhsd,bhse->bhde', wk, kc, vc)
        n = decay[..., None] * n + jnp.einsum('bhs,bhsd->bhd', wk, kc)
        return (C, n, m_new), h

    init = (jnp.zeros((B, H, dh, dh), jnp.float32), jnp.zeros((B, H, dh), jnp.float32),
            jnp.zeros((B, H), jnp.float32))
    _, h = lax.scan(step, init, (chunks(q), chunks(k), chunks(v), chunks(i_pre), chunks(log_f)))
    return jnp.moveaxis(h, 0, 2).reshape(B, H, S, dh)


def mlstm_branch(qa, ka, va, oa, ga, conv_w, conv_b, b_gates, norm_g):
    B, S, _ = qa.shape
    qk = jax.nn.silu(centred_depthwise_conv(jnp.concatenate([qa, ka], axis=-1), conv_w, conv_b))
    qa, ka = jnp.split(qk, 2, axis=-1)

    def heads(t):
        return t.reshape(B, S, A_HEADS, A_HEAD_DIM).transpose(0, 2, 1, 3).astype(jnp.float32)

    q, k, v = heads(qa), heads(ka), heads(va)
    gates = (ga.reshape(B, S, 4, A_HEADS) + b_gates.astype(ga.dtype)).transpose(2, 0, 3, 1).astype(jnp.float32)
    h_fwd = mlstm_direction(q, k, v, gates[0], gates[1])
    h_bwd = jnp.flip(mlstm_direction(jnp.flip(q, 2), jnp.flip(k, 2), jnp.flip(v, 2),
                                     jnp.flip(gates[2], -1), jnp.flip(gates[3], -1)), 2)
    h = (h_fwd + h_bwd).transpose(0, 2, 1, 3)
    h = rmsnorm(h, norm_g.reshape(A_HEADS, A_HEAD_DIM)).reshape(B, S, A_WIDTH)
    return (jax.nn.sigmoid(oa.astype(jnp.float32)) * h).astype(qa.dtype)


def axial_rope_tables(S):
    rows = S // GRID_W
    row_pos = jnp.repeat(jnp.arange(rows, dtype=jnp.float32), GRID_W)
    col_pos = jnp.tile(jnp.arange(GRID_W, dtype=jnp.float32), rows)
    half = B_HEAD_DIM // 2
    inv_freq = ROPE_THETA ** (-jnp.arange(0, half, 2, dtype=jnp.float32) / half)
    ang_r = row_pos[:, None] * inv_freq
    ang_c = col_pos[:, None] * inv_freq
    return (jnp.cos(ang_r)[:, None, :], jnp.sin(ang_r)[:, None, :],
            jnp.cos(ang_c)[:, None, :], jnp.sin(ang_c)[:, None, :])


def rope_rotate(x, cos, sin):
    x1, x2 = jnp.split(x, 2, axis=-1)
    return jnp.concatenate([x1 * cos - x2 * sin, x1 * sin + x2 * cos], axis=-1)


def apply_axial_rope(x, tables):
    cr, sr, cc, sc = tables
    xf = x.astype(jnp.float32)
    half = B_HEAD_DIM // 2
    y = jnp.concatenate([rope_rotate(xf[..., :half], cr, sr), rope_rotate(xf[..., half:], cc, sc)], axis=-1)
    return y.astype(x.dtype)


def attention_branch(qb, kb, vb, q_norm_g, k_norm_g):
    B, S, _ = qb.shape
    G = B_HEADS // B_KV_HEADS
    tables = axial_rope_tables(S)
    q = apply_axial_rope(rmsnorm(qb.reshape(B, S, B_HEADS, B_HEAD_DIM), q_norm_g), tables)
    k = apply_axial_rope(rmsnorm(kb.reshape(B, S, B_KV_HEADS, B_HEAD_DIM), k_norm_g), tables)
    v = vb.reshape(B, S, B_KV_HEADS, B_HEAD_DIM).transpose(0, 2, 1, 3)
    k = k.transpose(0, 2, 1, 3)
    q = q.reshape(B, S, B_KV_HEADS, G, B_HEAD_DIM).transpose(0, 2, 3, 1, 4)
    nb = S // Q_BLOCK
    q_blocks = jnp.moveaxis(q.reshape(B, B_KV_HEADS, G, nb, Q_BLOCK, B_HEAD_DIM), 3, 0)
    scale = B_HEAD_DIM ** -0.5

    def attend(q_blk):
        s = jnp.einsum('bkgqd,bksd->bkgqs', q_blk, k).astype(jnp.float32) * scale
        p = jax.nn.softmax(s, axis=-1)
        return jnp.einsum('bkgqs,bksd->bkgqd', p.astype(v.dtype), v)

    o = lax.map(attend, q_blocks)
    return o.transpose(1, 0, 4, 2, 3, 5).reshape(B, S, B_WIDTH)


def moe_ffn(x, w_router, b_router, w_gate_up, b_gate_up, w_down, b_down):
    B, S, D = x.shape
    N = B * S
    xt = x.reshape(N, D)
    logits = (xt @ w_router + b_router).astype(jnp.float32)
    top_v, top_i = lax.top_k(logits, TOP_K)
    gate_w = jax.nn.softmax(top_v, axis=-1).astype(x.dtype)
    e_flat = top_i.reshape(-1)
    w_flat = gate_w.reshape(-1)
    tok_flat = jnp.repeat(jnp.arange(N, dtype=jnp.int32), TOP_K)
    order = jnp.argsort(e_flat)
    e_sorted = e_flat[order]
    counts = jnp.zeros((N_EXPERTS,), jnp.int32).at[e_flat].add(1)
    padded = ((counts + MOE_BLOCK - 1) // MOE_BLOCK) * MOE_BLOCK
    pad_end = jnp.cumsum(padded)
    pad_start = pad_end - padded
    start = jnp.cumsum(counts) - counts
    dest = pad_start[e_sorted] + jnp.arange(N * TOP_K, dtype=jnp.int32) - start[e_sorted]
    P = N * TOP_K + N_EXPERTS * MOE_BLOCK
    n_blk = P // MOE_BLOCK
    row_tok = jnp.full((P,), N, jnp.int32).at[dest].set(tok_flat[order])
    row_w = jnp.zeros((P,), x.dtype).at[dest].set(w_flat[order])
    blk_expert = jnp.minimum(jnp.searchsorted(pad_end, jnp.arange(n_blk, dtype=jnp.int32) * MOE_BLOCK, side='right'),
                             N_EXPERTS - 1)
    x_pad = jnp.concatenate([xt, jnp.zeros((1, D), xt.dtype)], axis=0)
    xg = x_pad[row_tok].reshape(n_blk, MOE_BLOCK, D)

    def expert_block(args):
        xb, e = args
        hgu = xb @ w_gate_up[e] + b_gate_up[e]
        gate, up = jnp.split(hgu, 2, axis=-1)
        gate = jnp.minimum(gate, SWIGLU_LIMIT)
        up = jnp.clip(up, -SWIGLU_LIMIT, SWIGLU_LIMIT)
        glu = gate * jax.nn.sigmoid(gate * SWIGLU_ALPHA)
        return ((up + 1.0) * glu) @ w_down[e] + b_down[e]

    yg = lax.map(expert_block, (xg, blk_expert)).reshape(P, D) * row_w[:, None]
    y = jax.ops.segment_sum(yg, row_tok, num_segments=N + 1)[:N]
    return y.reshape(B, S, D)


def setup_inputs(seed: int = 0) -> dict:
    key = jax.random.key(seed)
    ks = jax.random.split(key, 24)
    L, D, E = DEPTH, D_MODEL, N_EXPERTS

    def nrm(k, shape, scale):
        return jax.random.normal(k, shape, jnp.float32) * scale

    def gain(k, shape):
        return 1.0 + 0.05 * jax.random.normal(k, shape, jnp.float32)

    gate_base = jnp.stack([jnp.zeros((A_HEADS,)), jnp.linspace(3.0, 6.0, A_HEADS),
                           jnp.zeros((A_HEADS,)), jnp.linspace(3.0, 6.0, A_HEADS)]).astype(jnp.float32)
    return {
        "x": nrm(ks[0], (BATCH, SEQ, D), 1.0),
        "norm_mix_g": gain(ks[1], (L, D)),
        "w_in": nrm(ks[2], (L, D, D_IN), D ** -0.5),
        "conv_w": nrm(ks[3], (L, CONV_W, 2 * A_WIDTH), CONV_W ** -0.5),
        "conv_b": nrm(ks[4], (L, 2 * A_WIDTH), 0.02),
        "b_mlstm_gates": gate_base[None] + nrm(ks[5], (L, 4, A_HEADS), 0.1),
        "mlstm_norm_g": gain(ks[6], (L, A_WIDTH)),
        "q_norm_g": gain(ks[7], (L, B_HEAD_DIM)),
        "k_norm_g": gain(ks[8], (L, B_HEAD_DIM)),
        "w_proj_a": nrm(ks[9], (L, A_WIDTH, D), A_WIDTH ** -0.5),
        "w_proj_b": nrm(ks[10], (L, B_WIDTH, D), B_WIDTH ** -0.5),
        "w_out": nrm(ks[11], (L, D, D), D ** -0.5),
        "norm_ffn_g": gain(ks[12], (L, D)),
        "w_router": nrm(ks[13], (L, D, E), D ** -0.5),
        "b_router": nrm(ks[14], (L, E), 0.01),
        "w_gate_up": nrm(ks[15], (L, E, D, 2 * D_FF), D ** -0.5),
        "b_gate_up": nrm(ks[16], (L, E, 2 * D_FF), 0.01),
        "w_down": nrm(ks[17], (L, E, D_FF, D), D_FF ** -0.5),
        "b_down": nrm(ks[18], (L, E, D), 0.01),
        "norm_final_g": gain(ks[19], (D,)),
    }


def reference(x, norm_mix_g, w_in, conv_w, conv_b, b_mlstm_gates, mlstm_norm_g, q_norm_g, k_norm_g,
              w_proj_a, w_proj_b, w_out, norm_ffn_g, w_router, b_router, w_gate_up, b_gate_up,
              w_down, b_down, norm_final_g):
    split_idx = [int(s) for s in np.cumsum(SPLITS)[:-1]]
    h = x
    for l in range(DEPTH):
        u = rmsnorm(h, norm_mix_g[l])
        z = u @ w_in[l]
        qa, ka, va, oa, ga, qb, kb, vb, gm = jnp.split(z, split_idx, axis=-1)
        ya = mlstm_branch(qa, ka, va, oa, ga, conv_w[l], conv_b[l], b_mlstm_gates[l], mlstm_norm_g[l])
        yb = attention_branch(qb, kb, vb, q_norm_g[l], k_norm_g[l])
        g_a, g_b = jnp.split(gm, 2, axis=-1)
        mixed = jax.nn.sigmoid(g_a) * (ya @ w_proj_a[l]) + jax.nn.sigmoid(g_b) * (yb @ w_proj_b[l])
        h = h + mixed @ w_out[l]
        h = h + moe_ffn(rmsnorm(h, norm_ffn_g[l]), w_router[l], b_router[l], w_gate_up[l], b_gate_up[l],
                        w_down[l], b_down[l])
    return rmsnorm(h, norm_final_g)
```

```python
import functools

import jax
import jax.numpy as jnp
import numpy as np
from jax import lax
from jax.experimental import pallas as pl
from jax.experimental.pallas import tpu as pltpu

D_MODEL = 2048
SEQ = 8192
GRID_W = 64
NORM_EPS = 1e-6
A_HEADS = 4
A_HEAD_DIM = 256
A_WIDTH = A_HEADS * A_HEAD_DIM
CONV_W = 5
B_HEADS = 8
B_KV_HEADS = 2
B_HEAD_DIM = 128
B_WIDTH = B_HEADS * B_HEAD_DIM
B_KV_WIDTH = B_KV_HEADS * B_HEAD_DIM
ROPE_THETA = 10000.0
N_EXPERTS = 32
TOP_K = 4
D_FF = D_MODEL
SWIGLU_LIMIT = 7.0
SWIGLU_ALPHA = 1.702

LANES = 128
SUBLANES = 8
VMEM_LIMIT = 56 * 1024 * 1024

Z_QK = 0
Z_VA = 2048
Z_OA = 3072
Z_GA = 4096
Z_GB = 6144
Z_QB = 8192
Z_KB = 9216
Z_VB = 9472
Z_WIDTH = 9728

PROJ_TM, PROJ_TN = 1024, 512
CONV_R, CONV_C = 1024, 256
A_CHUNK = 128
PREP_TM = 512
ATT_TQ, ATT_TK = 128, 512
MIX_TM = 256
OUT_TM = 256
ROUTE_TC = 512
MOE_TM = 512
MOE_TF = 512
MOE_ROWS = SEQ * TOP_K + N_EXPERTS * MOE_TM
MOE_TILES = MOE_ROWS // MOE_TM
DISPATCH_GROUP = 32
COMB_TT = 256


def _cparams(sem, vmem=VMEM_LIMIT):
    return pltpu.CompilerParams(dimension_semantics=sem, vmem_limit_bytes=vmem)


def _sigmoid(x):
    return 1.0 / (1.0 + jnp.exp(-x))


def _dot(a, b):
    return jnp.dot(a, b, preferred_element_type=jnp.float32)


def _dot_nt(a, b):
    return lax.dot_general(a, b, (((1,), (1,)), ((), ())), preferred_element_type=jnp.float32)


def _in_proj_kernel(x_ref, g_ref, w_ref, wg_ref, z_ref, gate_ref, u_ref):
    @pl.when(pl.program_id(1) == 0)
    def _():
        x = x_ref[...]
        y = x * lax.rsqrt(jnp.mean(x * x, axis=-1, keepdims=True) + NORM_EPS) * g_ref[...]
        u = y.astype(jnp.bfloat16)
        u_ref[...] = u
        gate_ref[...] = _dot(u, wg_ref[...])

    z_ref[...] = _dot(u_ref[...], w_ref[...])


def _in_proj(x, g, w_main, w_gate):
    grid = (SEQ // PROJ_TM, Z_WIDTH // PROJ_TN)
    return pl.pallas_call(
        _in_proj_kernel,
        out_shape=(jax.ShapeDtypeStruct((SEQ, Z_WIDTH), jnp.float32),
                   jax.ShapeDtypeStruct((SEQ, LANES), jnp.float32)),
        grid=grid,
        in_specs=[pl.BlockSpec((PROJ_TM, D_MODEL), lambda i, j: (i, 0)),
                  pl.BlockSpec((1, D_MODEL), lambda i, j: (0, 0)),
                  pl.BlockSpec((D_MODEL, PROJ_TN), lambda i, j: (0, j)),
                  pl.BlockSpec((D_MODEL, LANES), lambda i, j: (0, 0))],
        out_specs=(pl.BlockSpec((PROJ_TM, PROJ_TN), lambda i, j: (i, j)),
                   pl.BlockSpec((PROJ_TM, LANES), lambda i, j: (i, 0))),
        scratch_shapes=[pltpu.VMEM((PROJ_TM, D_MODEL), jnp.bfloat16)],
        compiler_params=_cparams(("arbitrary", "arbitrary")),
        name="in_proj",
    )(x, g, w_main, w_gate)


def _conv_kernel(prev_ref, main_ref, next_ref, w_ref, b_ref, *out_refs, scale, transposed):
    i = pl.program_id(0)
    n = pl.num_programs(0)
    prev = jnp.where(i > 0, prev_ref[...], 0.0)
    nxt = jnp.where(i < n - 1, next_ref[...], 0.0)
    ext = jnp.concatenate([prev, main_ref[...], nxt], axis=0)
    rows = CONV_R + 2 * SUBLANES
    w = w_ref[...]
    acc = jnp.zeros((CONV_R, CONV_C), jnp.float32) + b_ref[...]
    for j in range(CONV_W):
        shift = (CONV_W // 2 - j) % rows
        xs = ext if shift == 0 else pltpu.roll(ext, shift, axis=0)
        acc = acc + xs[SUBLANES:SUBLANES + CONV_R, :] * w[j:j + 1, :]
    y = acc * _sigmoid(acc) * scale
    out_refs[0][...] = y.astype(jnp.bfloat16)
    if transposed:
        out_refs[1][...] = y.T.astype(jnp.bfloat16)


def _conv(z, conv_w, conv_b, col_block0, scale, transposed):
    nr, ncb = SEQ // CONV_R, A_WIDTH // CONV_C
    rb = CONV_R // SUBLANES
    last = SEQ // SUBLANES - 1
    out_shape = [jax.ShapeDtypeStruct((SEQ, A_WIDTH), jnp.bfloat16)]
    out_specs = [pl.BlockSpec((CONV_R, CONV_C), lambda i, c: (i, c))]
    if transposed:
        out_shape.append(jax.ShapeDtypeStruct((A_WIDTH, SEQ), jnp.bfloat16))
        out_specs.append(pl.BlockSpec((CONV_C, CONV_R), lambda i, c: (c, i)))
    return pl.pallas_call(
        functools.partial(_conv_kernel, scale=scale, transposed=transposed),
        out_shape=tuple(out_shape),
        grid=(nr, ncb),
        in_specs=[pl.BlockSpec((SUBLANES, CONV_C), lambda i, c: (jnp.maximum(i * rb - 1, 0), col_block0 + c)),
                  pl.BlockSpec((CONV_R, CONV_C), lambda i, c: (i, col_block0 + c)),
                  pl.BlockSpec((SUBLANES, CONV_C), lambda i, c: (jnp.minimum((i + 1) * rb, last), col_block0 + c)),
                  pl.BlockSpec((CONV_W, CONV_C), lambda i, c: (0, col_block0 + c)),
                  pl.BlockSpec((1, CONV_C), lambda i, c: (0, col_block0 + c))],
        out_specs=tuple(out_specs),
        compiler_params=_cparams(("arbitrary", "arbitrary")),
        name="conv_k" if transposed else "conv_q",
    )(z, z, z, conv_w, conv_b)


def _log_sigmoid(x):
    return jnp.minimum(x, 0.0) - jnp.log(1.0 + jnp.exp(-jnp.abs(x)))


def _cumsum_rows(x):
    n = x.shape[0]
    row = lax.broadcasted_iota(jnp.int32, x.shape, 0)
    s = 1
    while s < n:
        x = x + jnp.where(row >= s, pltpu.roll(x, s, axis=0), 0.0)
        s *= 2
    return x


def _mlstm_gate_tables(g_ref, bias_ref, h, backward):
    L = A_CHUNK
    pre = g_ref[...] + bias_ref[...]
    logf = _log_sigmoid(pre)
    pref = _cumsum_rows(logf)
    total = pref[L - 1:L, :]
    cum = (total - pref + logf) if backward else pref
    col_i = (8 if backward else 0) + h
    col_f = (12 if backward else 4) + h
    lane = lax.broadcasted_iota(jnp.int32, (L, LANES), 1)
    tab = jnp.where(lane == col_f, cum, pre)
    tab_t = tab.T
    sub = lax.broadcasted_iota(jnp.int32, (LANES, L), 0)

    def col(k):
        return jnp.sum(jnp.where(lane == k, tab, 0.0), axis=1, keepdims=True)

    def row(k):
        return jnp.sum(jnp.where(sub == k, tab_t, 0.0), axis=0, keepdims=True)

    lane1 = lax.broadcasted_iota(jnp.int32, (1, LANES), 1)
    b_last = jnp.sum(jnp.where(lane1 == col_f, total, 0.0), axis=1, keepdims=True)
    return col(col_i), col(col_f), row(col_i), row(col_f), b_last


def _mlstm_chain(q_ref, k_ref, kt_ref, v_ref, g_ref, bias_ref, h_ref, c_ref, n_ref, m_ref, h, backward):
    L = A_CHUNK
    i_c, b_c, i_r, b_r, b_last = _mlstm_gate_tables(g_ref, bias_ref, h, backward)
    q = q_ref[...]
    k = k_ref[...]
    v = v_ref[...].astype(jnp.bfloat16)
    t_idx = lax.broadcasted_iota(jnp.int32, (L, L), 0)
    s_idx = lax.broadcasted_iota(jnp.int32, (L, L), 1)
    visible = (s_idx >= t_idx) if backward else (s_idx <= t_idx)
    d = jnp.where(visible, b_c - b_r + i_r, -jnp.inf)
    m_prev = m_ref[0:1, 0:1]
    m_inter = b_c + m_prev
    m_t = jnp.maximum(m_inter, jnp.max(d, axis=1, keepdims=True))
    w = jnp.exp(d - m_t) * _dot_nt(q, k)
    s_inter = jnp.exp(m_inter - m_t)
    c_old = c_ref[...]
    n_old = n_ref[0:1, :]
    num = _dot(w.astype(jnp.bfloat16), v) + s_inter * _dot(q, c_old.astype(jnp.bfloat16))
    qn = jnp.sum(q.astype(jnp.float32) * n_old, axis=1, keepdims=True)
    den = jnp.sum(w, axis=1, keepdims=True) + s_inter * qn
    h_ref[...] = num / jnp.maximum(jnp.abs(den), jnp.exp(-m_t))

    g_c = b_last - b_c + i_c
    g_r = b_last - b_r + i_r
    m_new = jnp.maximum(b_last + m_prev, jnp.max(g_r, axis=1, keepdims=True))
    wk_c = jnp.exp(g_c - m_new)
    wk_r = jnp.exp(g_r - m_new)
    decay = jnp.exp(b_last + m_prev - m_new)
    ktw = (kt_ref[...].astype(jnp.float32) * wk_r).astype(jnp.bfloat16)
    c_ref[...] = decay * c_old + _dot(ktw, v)
    n_new = decay * n_old + jnp.sum(k.astype(jnp.float32) * wk_c, axis=0, keepdims=True)
    n_ref[...] = jnp.broadcast_to(n_new, n_ref.shape)
    m_ref[...] = jnp.broadcast_to(m_new, m_ref.shape)


def _mlstm_kernel(qf, kf, ktf, vf, gf, qb, kb, ktb, vb, gb, bias_ref, hf_ref, hb_ref,
                  cf_ref, nf_ref, mf_ref, cb_ref, nb_ref, mb_ref):
    h = pl.program_id(0)

    @pl.when(pl.program_id(1) == 0)
    def _():
        for r in (cf_ref, nf_ref, mf_ref, cb_ref, nb_ref, mb_ref):
            r[...] = jnp.zeros(r.shape, r.dtype)

    _mlstm_chain(qf, kf, ktf, vf, gf, bias_ref, hf_ref, cf_ref, nf_ref, mf_ref, h, False)
    _mlstm_chain(qb, kb, ktb, vb, gb, bias_ref, hb_ref, cb_ref, nb_ref, mb_ref, h, True)


def _mlstm(qc, kc, kct, z, gates, bias):
    L, dh = A_CHUNK, A_HEAD_DIM
    nc = SEQ // L
    va_blk = Z_VA // dh

    def specs(chunk):
        return [pl.BlockSpec((L, dh), lambda h, j: (chunk(j), h)),
                pl.BlockSpec((L, dh), lambda h, j: (chunk(j), h)),
                pl.BlockSpec((dh, L), lambda h, j: (h, chunk(j))),
                pl.BlockSpec((L, dh), lambda h, j: (chunk(j), va_blk + h)),
                pl.BlockSpec((L, LANES), lambda h, j: (chunk(j), 0))]

    fwd = lambda j: j
    bwd = lambda j: nc - 1 - j
    state = [pltpu.VMEM((dh, dh), jnp.float32), pltpu.VMEM((SUBLANES, dh), jnp.float32),
             pltpu.VMEM((SUBLANES, LANES), jnp.float32)]
    return pl.pallas_call(
        _mlstm_kernel,
        out_shape=(jax.ShapeDtypeStruct((SEQ, A_WIDTH), jnp.float32),
                   jax.ShapeDtypeStruct((SEQ, A_WIDTH), jnp.float32)),
        grid=(A_HEADS, nc),
        in_specs=specs(fwd) + specs(bwd) + [pl.BlockSpec((1, LANES), lambda h, j: (0, 0))],
        out_specs=(pl.BlockSpec((L, dh), lambda h, j: (fwd(j), h)),
                   pl.BlockSpec((L, dh), lambda h, j: (bwd(j), h))),
        scratch_shapes=state + state,
        compiler_params=_cparams(("arbitrary", "arbitrary")),
        name="mlstm",
    )(qc, kc, kct, z, gates, qc, kc, kct, z, gates, bias)


def _attn_prep_kernel(q_ref, k_ref, v_ref, qg_ref, kg_ref, invf_ref, qo_ref, ko_ref, vo_ref):
    tm = q_ref.shape[0]
    t = pl.program_id(0) * tm + lax.broadcasted_iota(jnp.int32, (tm, LANES), 0)
    lane = lax.broadcasted_iota(jnp.int32, (tm, LANES), 1)
    half = B_HEAD_DIM // 2
    grid_shift = GRID_W.bit_length() - 1
    pos = jnp.where(lane < half, t >> grid_shift, t & (GRID_W - 1)).astype(jnp.float32)
    ang = pos * invf_ref[...]
    cos = jnp.cos(ang)
    first = (lane & (half - 1)) < (half // 2)
    sin = jnp.where(first, -jnp.sin(ang), jnp.sin(ang))

    def norm_rope(x, g, scale):
        y = x * lax.rsqrt(jnp.mean(x * x, axis=-1, keepdims=True) + NORM_EPS) * g
        partner = jnp.where(first, pltpu.roll(y, LANES - half // 2, axis=1), pltpu.roll(y, half // 2, axis=1))
        return ((y * cos + partner * sin) * scale).astype(jnp.bfloat16)

    for hd in range(B_HEADS):
        sl = slice(hd * B_HEAD_DIM, (hd + 1) * B_HEAD_DIM)
        qo_ref[:, sl] = norm_rope(q_ref[:, sl], qg_ref[...], B_HEAD_DIM ** -0.5)
    for hd in range(B_KV_HEADS):
        sl = slice(hd * B_HEAD_DIM, (hd + 1) * B_HEAD_DIM)
        ko_ref[:, sl] = norm_rope(k_ref[:, sl], kg_ref[...], 1.0)
    vo_ref[...] = v_ref[...].astype(jnp.bfloat16)


def _attn_prep(z, q_norm_g, k_norm_g, inv_freq):
    tm = PREP_TM
    return pl.pallas_call(
        _attn_prep_kernel,
        out_shape=(jax.ShapeDtypeStruct((SEQ, B_WIDTH), jnp.bfloat16),
                   jax.ShapeDtypeStruct((SEQ, B_KV_WIDTH), jnp.bfloat16),
                   jax.ShapeDtypeStruct((SEQ, B_KV_WIDTH), jnp.bfloat16)),
        grid=(SEQ // tm,),
        in_specs=[pl.BlockSpec((tm, B_WIDTH), lambda i: (i, Z_QB // B_WIDTH)),
                  pl.BlockSpec((tm, B_KV_WIDTH), lambda i: (i, Z_KB // B_KV_WIDTH)),
                  pl.BlockSpec((tm, B_KV_WIDTH), lambda i: (i, Z_VB // B_KV_WIDTH)),
                  pl.BlockSpec((1, LANES), lambda i: (0, 0)),
                  pl.BlockSpec((1, LANES), lambda i: (0, 0)),
                  pl.BlockSpec((1, LANES), lambda i: (0, 0))],
        out_specs=(pl.BlockSpec((tm, B_WIDTH), lambda i: (i, 0)),
                   pl.BlockSpec((tm, B_KV_WIDTH), lambda i: (i, 0)),
                   pl.BlockSpec((tm, B_KV_WIDTH), lambda i: (i, 0))),
        compiler_params=_cparams(("arbitrary",)),
        name="attn_prep",
    )(z, z, z, q_norm_g, k_norm_g, inv_freq)


def _attn_kernel(q_ref, k_ref, v_ref, o_ref, m_ref, l_ref, acc_ref):
    group = B_HEADS // B_KV_HEADS
    q = jnp.concatenate([q_ref[:, g * B_HEAD_DIM:(g + 1) * B_HEAD_DIM] for g in range(group)], axis=0)
    m_ref[...] = jnp.full(m_ref.shape, -jnp.inf, jnp.float32)
    l_ref[...] = jnp.zeros(l_ref.shape, jnp.float32)
    acc_ref[...] = jnp.zeros(acc_ref.shape, jnp.float32)

    def body(c, carry):
        off = pl.multiple_of(c * ATT_TK, ATT_TK)
        s = _dot_nt(q, k_ref[pl.ds(off, ATT_TK), :])
        m_old = m_ref[...]
        m_new = jnp.maximum(m_old, jnp.max(s, axis=1, keepdims=True))
        p = jnp.exp(s - m_new)
        alpha = jnp.exp(m_old - m_new)
        l_ref[...] = alpha * l_ref[...] + jnp.sum(p, axis=1, keepdims=True)
        acc_ref[...] = alpha * acc_ref[...] + _dot(p.astype(jnp.bfloat16), v_ref[pl.ds(off, ATT_TK), :])
        m_ref[...] = m_new
        return carry

    lax.fori_loop(0, SEQ // ATT_TK, body, 0)
    out = acc_ref[...] / l_ref[...]
    for g in range(group):
        o_ref[:, g * B_HEAD_DIM:(g + 1) * B_HEAD_DIM] = out[g * ATT_TQ:(g + 1) * ATT_TQ].astype(o_ref.dtype)


def _attn(qr, kr, vr):
    group = B_HEADS // B_KV_HEADS
    rows = group * ATT_TQ
    return pl.pallas_call(
        _attn_kernel,
        out_shape=jax.ShapeDtypeStruct((SEQ, B_WIDTH), jnp.bfloat16),
        grid=(B_KV_HEADS, SEQ // ATT_TQ),
        in_specs=[pl.BlockSpec((ATT_TQ, group * B_HEAD_DIM), lambda kv, i: (i, kv)),
                  pl.BlockSpec((SEQ, B_HEAD_DIM), lambda kv, i: (0, kv)),
                  pl.BlockSpec((SEQ, B_HEAD_DIM), lambda kv, i: (0, kv))],
        out_specs=pl.BlockSpec((ATT_TQ, group * B_HEAD_DIM), lambda kv, i: (i, kv)),
        scratch_shapes=[pltpu.VMEM((rows, 1), jnp.float32), pltpu.VMEM((rows, 1), jnp.float32),
                        pltpu.VMEM((rows, B_HEAD_DIM), jnp.float32)],
        compiler_params=_cparams(("arbitrary", "arbitrary")),
        name="attn",
    )(qr, kr, vr)


def _mix_kernel(hf_ref, hb_ref, oa_ref, ng_ref, yb_ref, ga_ref, gb_ref, pa_ref, pb_ref, o_ref):
    parts = []
    for hd in range(A_HEADS):
        sl = slice(hd * A_HEAD_DIM, (hd + 1) * A_HEAD_DIM)
        hs = hf_ref[:, sl] + hb_ref[:, sl]
        y = hs * lax.rsqrt(jnp.mean(hs * hs, axis=-1, keepdims=True) + NORM_EPS) * ng_ref[:, sl]
        parts.append((_sigmoid(oa_ref[:, sl]) * y).astype(jnp.bfloat16))
    ya = jnp.concatenate(parts, axis=1)
    ta = _dot(ya, pa_ref[...])
    tb = _dot(yb_ref[...], pb_ref[...])
    o_ref[...] = (_sigmoid(ga_ref[...]) * ta + _sigmoid(gb_ref[...]) * tb).astype(o_ref.dtype)


def _mix(hf, hb, z, norm_g, yb, pa, pb):
    tm = MIX_TM
    return pl.pallas_call(
        _mix_kernel,
        out_shape=jax.ShapeDtypeStruct((SEQ, D_MODEL), jnp.bfloat16),
        grid=(SEQ // tm,),
        in_specs=[pl.BlockSpec((tm, A_WIDTH), lambda i: (i, 0)),
                  pl.BlockSpec((tm, A_WIDTH), lambda i: (i, 0)),
                  pl.BlockSpec((tm, A_WIDTH), lambda i: (i, Z_OA // A_WIDTH)),
                  pl.BlockSpec((1, A_WIDTH), lambda i: (0, 0)),
                  pl.BlockSpec((tm, B_WIDTH), lambda i: (i, 0)),
                  pl.BlockSpec((tm, D_MODEL), lambda i: (i, Z_GA // D_MODEL)),
                  pl.BlockSpec((tm, D_MODEL), lambda i: (i, Z_GB // D_MODEL)),
                  pl.BlockSpec((A_WIDTH, D_MODEL), lambda i: (0, 0)),
                  pl.BlockSpec((B_WIDTH, D_MODEL), lambda i: (0, 0))],
        out_specs=pl.BlockSpec((tm, D_MODEL), lambda i: (i, 0)),
        compiler_params=_cparams(("arbitrary",)),
        name="mix",
    )(hf, hb, z, norm_g, yb, z, z, pa, pb)


def _split_bf16(a):
    hi = a.astype(jnp.bfloat16)
    lo = (a - hi.astype(jnp.float32)).astype(jnp.bfloat16)
    return hi, lo


def _out_proj_kernel(mx_ref, w_ref, x_ref, g_ref, wr_ref, br_ref, h1_ref, xw_ref, lg_ref):
    h1 = x_ref[...] + _dot(mx_ref[...], w_ref[...])
    h1_ref[...] = h1
    xn = h1 * lax.rsqrt(jnp.mean(h1 * h1, axis=-1, keepdims=True) + NORM_EPS) * g_ref[...]
    x_hi, x_lo = _split_bf16(xn)
    w_hi, w_lo = _split_bf16(wr_ref[...])
    lg_ref[...] = _dot_nt(w_hi, x_hi) + (_dot_nt(w_hi, x_lo) + _dot_nt(w_lo, x_hi)) + br_ref[...]
    bits = pltpu.bitcast(x_hi.astype(jnp.float32), jnp.uint32)
    half = D_MODEL // 2
    xw_ref[...] = (bits[:, half:] & jnp.uint32(0xFFFF0000)) | (bits[:, :half] >> 16)


def _out_proj(mixed, w_out, x, g, w_router_t, b_router):
    tm = OUT_TM
    return pl.pallas_call(
        _out_proj_kernel,
        out_shape=(jax.ShapeDtypeStruct((SEQ, D_MODEL), jnp.float32),
                   jax.ShapeDtypeStruct((SEQ, D_MODEL // 2), jnp.uint32),
                   jax.ShapeDtypeStruct((N_EXPERTS, SEQ), jnp.float32)),
        grid=(SEQ // tm,),
        in_specs=[pl.BlockSpec((tm, D_MODEL), lambda i: (i, 0)),
                  pl.BlockSpec((D_MODEL, D_MODEL), lambda i: (0, 0)),
                  pl.BlockSpec((tm, D_MODEL), lambda i: (i, 0)),
                  pl.BlockSpec((1, D_MODEL), lambda i: (0, 0)),
                  pl.BlockSpec((N_EXPERTS, D_MODEL), lambda i: (0, 0)),
                  pl.BlockSpec((N_EXPERTS, 1), lambda i: (0, 0))],
        out_specs=(pl.BlockSpec((tm, D_MODEL), lambda i: (i, 0)),
                   pl.BlockSpec((tm, D_MODEL // 2), lambda i: (i, 0)),
                   pl.BlockSpec((N_EXPERTS, tm), lambda i: (0, i))),
        compiler_params=_cparams(("arbitrary",)),
        name="out_proj",
    )(mixed, w_out, x, g, w_router_t, b_router)


def _route_kernel(lg_ref, e_ref, w_ref, r_ref, cnt_ref, carry_ref):
    tc = ROUTE_TC
    carry_ref[...] = jnp.zeros(carry_ref.shape, jnp.float32)
    eidx = lax.broadcasted_iota(jnp.int32, (N_EXPERTS, tc), 0)
    tri_r = lax.broadcasted_iota(jnp.int32, (tc, tc), 0)
    tri_c = lax.broadcasted_iota(jnp.int32, (tc, tc), 1)
    before = jnp.where(tri_r < tri_c, 1.0, 0.0).astype(jnp.bfloat16)

    def body(c, carry):
        off = pl.multiple_of(c * tc, tc)
        lg = lg_ref[:, pl.ds(off, tc)]
        vals, sels = [], []
        for _ in range(TOP_K):
            mx = jnp.max(lg, axis=0, keepdims=True)
            idx = jnp.min(jnp.where(lg == mx, eidx, N_EXPERTS), axis=0, keepdims=True)
            sel = eidx == idx
            vals.append(mx)
            sels.append(sel)
            lg = jnp.where(sel, -jnp.inf, lg)
        ex = [jnp.exp(v - vals[0]) for v in vals]
        tot = ex[0] + ex[1] + ex[2] + ex[3]
        chosen = jnp.where(sels[0] | sels[1] | sels[2] | sels[3], 1.0, 0.0)
        rank = _dot(chosen.astype(jnp.bfloat16), before) + carry_ref[:, 0:1]
        carry_ref[...] = carry_ref[...] + jnp.sum(chosen, axis=1, keepdims=True)
        for k in range(TOP_K):
            e_ref[k:k + 1, pl.ds(off, tc)] = jnp.sum(jnp.where(sels[k], eidx, 0), axis=0, keepdims=True)
            w_ref[k:k + 1, pl.ds(off, tc)] = ex[k] / tot
            r_ref[k:k + 1, pl.ds(off, tc)] = jnp.sum(jnp.where(sels[k], rank, 0.0), axis=0,
                                                     keepdims=True).astype(jnp.int32)
        return carry

    lax.fori_loop(0, SEQ // tc, body, 0)
    cnt_ref[...] = carry_ref[...].astype(jnp.int32)


def _route(logits_t):
    return pl.pallas_call(
        _route_kernel,
        out_shape=(jax.ShapeDtypeStruct((TOP_K, SEQ), jnp.int32),
                   jax.ShapeDtypeStruct((TOP_K, SEQ), jnp.float32),
                   jax.ShapeDtypeStruct((TOP_K, SEQ), jnp.int32),
                   jax.ShapeDtypeStruct((N_EXPERTS, LANES), jnp.int32)),
        scratch_shapes=[pltpu.VMEM((N_EXPERTS, LANES), jnp.float32)],
        compiler_params=_cparams(None),
        name="route",
    )(logits_t)


def _dispatch_kernel(pos_ref, xw_ref, buf_ref, xg_ref, sem):
    del buf_ref
    group = DISPATCH_GROUP

    def copy(t, k):
        return pltpu.make_async_copy(xw_ref.at[pl.ds(t, 1)], xg_ref.at[pl.ds(pos_ref[k * SEQ + t], 1)], sem)

    def start_group(gi):
        def body(j, c):
            for k in range(TOP_K):
                copy(gi * group + j, k).start()
            return c
        lax.fori_loop(0, group, body, 0)

    def wait_group(gi):
        def body(j, c):
            for k in range(TOP_K):
                copy(gi * group + j, k).wait()
            return c
        lax.fori_loop(0, group, body, 0)

    n_groups = SEQ // group
    start_group(0)

    def outer(gi, c):
        start_group(gi)
        wait_group(gi - 1)
        return c

    lax.fori_loop(1, n_groups, outer, 0)
    wait_group(n_groups - 1)


def _dispatch(pos_flat, xw, xg_init):
    return pl.pallas_call(
        _dispatch_kernel,
        out_shape=jax.ShapeDtypeStruct((MOE_ROWS, D_MODEL // 2), jnp.uint32),
        grid_spec=pltpu.PrefetchScalarGridSpec(
            num_scalar_prefetch=1,
            grid=(1,),
            in_specs=[pl.BlockSpec(memory_space=pl.ANY), pl.BlockSpec(memory_space=pl.ANY)],
            out_specs=pl.BlockSpec(memory_space=pl.ANY),
            scratch_shapes=[pltpu.SemaphoreType.DMA(())]),
        input_output_aliases={2: 0},
        compiler_params=_cparams(("arbitrary",)),
        name="dispatch",
    )(pos_flat, xw, xg_init)


def _experts_kernel(te_ref, na_ref, xg_ref, wg_ref, wu_ref, bg_ref, bu_ref, wd_ref, bd_ref, y_ref, x_ref):
    t = pl.program_id(0)
    f = pl.program_id(1)

    @pl.when(t < na_ref[0])
    def _():
        half = D_MODEL // 2

        @pl.when(f == 0)
        def _():
            words = xg_ref[...]
            lo = pltpu.bitcast(words << 16, jnp.float32)
            hi = pltpu.bitcast(words & jnp.uint32(0xFFFF0000), jnp.float32)
            x_ref[:, :half] = lo.astype(jnp.bfloat16)
            x_ref[:, half:] = hi.astype(jnp.bfloat16)
            y_ref[...] = jnp.zeros(y_ref.shape, jnp.float32) + bd_ref[...]

        x = x_ref[...]
        gate = _dot(x, wg_ref[...].astype(jnp.bfloat16)) + bg_ref[...]
        up = _dot(x, wu_ref[...].astype(jnp.bfloat16)) + bu_ref[...]
        gate = jnp.minimum(gate, SWIGLU_LIMIT)
        up = jnp.clip(up, -SWIGLU_LIMIT, SWIGLU_LIMIT)
        act = (up + 1.0) * (gate * _sigmoid(gate * SWIGLU_ALPHA))
        y_ref[...] += _dot(act.astype(jnp.bfloat16), wd_ref[...].astype(jnp.bfloat16))


def _experts(tile_expert, n_active, xg, w_gate_up, b_gate_up, w_down, b_down):
    nf = D_FF // MOE_TF
    tm, tf = MOE_TM, MOE_TF

    def tile(t, na):
        return jnp.minimum(t, na[0] - 1)

    def fcol(t, f, na):
        return jnp.where(t < na[0], f, nf - 1)

    def expert(t, te, na):
        return te[tile(t, na)]

    return pl.pallas_call(
        _experts_kernel,
        out_shape=jax.ShapeDtypeStruct((MOE_ROWS, D_MODEL), jnp.float32),
        grid_spec=pltpu.PrefetchScalarGridSpec(
            num_scalar_prefetch=2,
            grid=(MOE_TILES, nf),
            in_specs=[
                pl.BlockSpec((tm, D_MODEL // 2), lambda t, f, te, na: (tile(t, na), 0)),
                pl.BlockSpec((None, None, D_MODEL, tf), lambda t, f, te, na: (0, expert(t, te, na), 0, fcol(t, f, na))),
                pl.BlockSpec((None, None, D_MODEL, tf),
                             lambda t, f, te, na: (0, expert(t, te, na), 0, nf + fcol(t, f, na))),
                pl.BlockSpec((None, None, 1, tf), lambda t, f, te, na: (0, expert(t, te, na), 0, fcol(t, f, na))),
                pl.BlockSpec((None, None, 1, tf), lambda t, f, te, na: (0, expert(t, te, na), 0, nf + fcol(t, f, na))),
                pl.BlockSpec((None, None, tf, D_MODEL), lambda t, f, te, na: (0, expert(t, te, na), fcol(t, f, na), 0)),
                pl.BlockSpec((None, None, 1, D_MODEL), lambda t, f, te, na: (0, expert(t, te, na), 0, 0)),
            ],
            out_specs=pl.BlockSpec((tm, D_MODEL), lambda t, f, te, na: (tile(t, na), 0)),
            scratch_shapes=[pltpu.VMEM((tm, D_MODEL), jnp.bfloat16)]),
        compiler_params=_cparams(("arbitrary", "arbitrary")),
        name="experts",
    )(tile_expert, n_active, xg, w_gate_up, w_gate_up, b_gate_up, b_gate_up, w_down, b_down)


def _combine_kernel(pos_ref, yg_ref, w_ref, h1_ref, g_ref, o_ref, buf_ref, sem):
    tt = COMB_TT
    base = pl.program_id(0) * tt

    def copy(j, k):
        return pltpu.make_async_copy(yg_ref.at[pl.ds(pos_ref[k * SEQ + base + j], 1)],
                                     buf_ref.at[k, pl.ds(j, 1)], sem)

    def start(j, c):
        for k in range(TOP_K):
            copy(j, k).start()
        return c

    def wait(j, c):
        for k in range(TOP_K):
            copy(j, k).wait()
        return c

    lax.fori_loop(0, tt, start, 0)
    lax.fori_loop(0, tt, wait, 0)
    w = w_ref[...]
    h2 = h1_ref[...]
    for k in range(TOP_K):
        h2 = h2 + w[:, k:k + 1] * buf_ref[k]
    o_ref[...] = h2 * lax.rsqrt(jnp.mean(h2 * h2, axis=-1, keepdims=True) + NORM_EPS) * g_ref[...]


def _combine(pos_flat, yg, gate_w, h1, g):
    tt = COMB_TT
    return pl.pallas_call(
        _combine_kernel,
        out_shape=jax.ShapeDtypeStruct((SEQ, D_MODEL), jnp.float32),
        grid_spec=pltpu.PrefetchScalarGridSpec(
            num_scalar_prefetch=1,
            grid=(SEQ // tt,),
            in_specs=[pl.BlockSpec(memory_space=pl.ANY),
                      pl.BlockSpec((tt, TOP_K), lambda i, pos: (i, 0)),
                      pl.BlockSpec((tt, D_MODEL), lambda i, pos: (i, 0)),
                      pl.BlockSpec((1, D_MODEL), lambda i, pos: (0, 0))],
            out_specs=pl.BlockSpec((tt, D_MODEL), lambda i, pos: (i, 0)),
            scratch_shapes=[pltpu.VMEM((TOP_K, tt, D_MODEL), jnp.float32), pltpu.SemaphoreType.DMA(())]),
        compiler_params=_cparams(("arbitrary",)),
        name="combine",
    )(pos_flat, yg, gate_w, h1, g)


def _regroup_w_in(w):
    qa_oa = w[:, :4 * A_WIDTH]
    g0 = 4 * A_WIDTH
    gates = w[:, g0:g0 + 4 * A_HEADS]
    b0 = g0 + 4 * A_HEADS
    qb = w[:, b0:b0 + B_WIDTH]
    kv = w[:, b0 + B_WIDTH:b0 + B_WIDTH + 2 * B_KV_WIDTH]
    gm = w[:, b0 + B_WIDTH + 2 * B_KV_WIDTH:]
    main = jnp.concatenate([qa_oa, gm, qb, kv], axis=1).astype(jnp.bfloat16)
    gates = jnp.pad(gates, ((0, 0), (0, LANES - 4 * A_HEADS))).astype(jnp.bfloat16)
    return main, gates


def kernel(x, norm_mix_g, w_in, conv_w, conv_b, b_mlstm_gates, mlstm_norm_g, q_norm_g, k_norm_g, w_proj_a,
           w_proj_b, w_out, norm_ffn_g, w_router, b_router, w_gate_up, b_gate_up, w_down, b_down, norm_final_g):
    assert x.shape == (1, SEQ, D_MODEL) and w_in.shape[0] == 1
    x2 = x[0]
    w_main, w_gates = _regroup_w_in(w_in[0])
    z, gates = _in_proj(x2, norm_mix_g, w_main, w_gates)

    (qc,) = _conv(z, conv_w[0], conv_b, 0, A_HEAD_DIM ** -0.5, False)
    kc, kct = _conv(z, conv_w[0], conv_b, A_WIDTH // CONV_C, 1.0, True)
    gate_bias = jnp.pad(b_mlstm_gates.reshape(1, 4 * A_HEADS), ((0, 0), (0, LANES - 4 * A_HEADS)))
    hf, hb = _mlstm(qc, kc, kct, z, gates, gate_bias)

    half = B_HEAD_DIM // 2
    inv_freq = ROPE_THETA ** (-jnp.arange(0, half, 2, dtype=jnp.float32) / half)
    qr, kr, vr = _attn_prep(z, q_norm_g, k_norm_g, jnp.tile(inv_freq, 4).reshape(1, LANES))
    yb = _attn(qr, kr, vr)

    mixed = _mix(hf, hb, z, mlstm_norm_g, yb, w_proj_a[0].astype(jnp.bfloat16), w_proj_b[0].astype(jnp.bfloat16))
    h1, xw, logits_t = _out_proj(mixed, w_out[0].astype(jnp.bfloat16), x2, norm_ffn_g, w_router[0].T,
                                 b_router.reshape(N_EXPERTS, 1))

    top_e, gate_w, rank, counts = _route(logits_t)
    counts = counts[:, 0]
    padded = ((counts + MOE_TM - 1) // MOE_TM) * MOE_TM
    pad_end = jnp.cumsum(padded)
    pad_start = pad_end - padded
    pos = (pad_start[top_e] + rank).reshape(-1)
    n_active = (pad_end[-1] // MOE_TM).astype(jnp.int32).reshape(1)
    tile_start = jnp.arange(MOE_TILES, dtype=jnp.int32) * MOE_TM
    tile_expert = jnp.minimum(jnp.searchsorted(pad_end, tile_start, side="right"), N_EXPERTS - 1).astype(jnp.int32)

    xg = _dispatch(pos, xw, jnp.zeros((MOE_ROWS, D_MODEL // 2), jnp.uint32))
    yg = _experts(tile_expert, n_active, xg, w_gate_up, b_gate_up.reshape(1, N_EXPERTS, 1, 2 * D_FF),
                  w_down, b_down.reshape(1, N_EXPERTS, 1, D_MODEL))
    out = _combine(pos, yg, gate_w.T, h1, norm_final_g.reshape(1, D_MODEL))
    return out[None]
```

```python
import functools

import jax
import jax.numpy as jnp
import numpy as np
from jax import lax
from jax.experimental import pallas as pl
from jax.experimental.pallas import tpu as pltpu

D_MODEL = 2048
SEQ = 8192
GRID_W = 64
NORM_EPS = 1e-6
A_HEADS = 4
A_HEAD_DIM = 256
A_WIDTH = A_HEADS * A_HEAD_DIM
CONV_W = 5
B_HEADS = 8
B_KV_HEADS = 2
B_HEAD_DIM = 128
B_WIDTH = B_HEADS * B_HEAD_DIM
B_KV_WIDTH = B_KV_HEADS * B_HEAD_DIM
ROPE_THETA = 10000.0
N_EXPERTS = 32
TOP_K = 4
D_FF = D_MODEL
SWIGLU_LIMIT = 7.0
SWIGLU_ALPHA = 1.702
LOG2_E = 1.4426950408889634

LANES = 128
SUBLANES = 8
VMEM_LIMIT = 56 * 1024 * 1024

Z_QK = 0
Z_VA = 2048
Z_OA = 3072
Z_GA = 4096
Z_GB = 6144
Z_QB = 8192
Z_KB = 9216
Z_VB = 9472
Z_WIDTH = 9728

PROJ_TM, PROJ_TN = 1024, 512
CONV_R, CONV_C = 1024, 256
A_CHUNK = 256
PREP_TM = 512
ATT_TQ, ATT_TK = 128, 512
MIX_TM = 256
OUT_TM = 256
ROUTE_TC = 512
MOE_TM = 512
MOE_TF = 512
MOE_ROWS = SEQ * TOP_K + N_EXPERTS * MOE_TM
MOE_TILES = MOE_ROWS // MOE_TM
DISP_TT = 512
COMB_TT = 256


def _cparams(sem, vmem=VMEM_LIMIT):
    return pltpu.CompilerParams(dimension_semantics=sem, vmem_limit_bytes=vmem)


def _sigmoid(x):
    return 1.0 / (1.0 + jnp.exp(-x))


def _dot(a, b):
    return jnp.dot(a, b, preferred_element_type=jnp.float32)


def _dot_nt(a, b):
    return lax.dot_general(a, b, (((1,), (1,)), ((), ())), preferred_element_type=jnp.float32)


def _in_proj_kernel(x_ref, g_ref, w_ref, wg_ref, z_ref, gate_ref, u_ref):
    @pl.when(pl.program_id(1) == 0)
    def _():
        x = x_ref[...]
        y = x * lax.rsqrt(jnp.mean(x * x, axis=-1, keepdims=True) + NORM_EPS) * g_ref[...]
        u = y.astype(jnp.bfloat16)
        u_ref[...] = u
        gate_ref[...] = _dot(u, wg_ref[...])

    z_ref[...] = _dot(u_ref[...], w_ref[...])


def _in_proj(x, g, w_main, w_gate):
    grid = (SEQ // PROJ_TM, Z_WIDTH // PROJ_TN)
    return pl.pallas_call(
        _in_proj_kernel,
        out_shape=(jax.ShapeDtypeStruct((SEQ, Z_WIDTH), jnp.float32),
                   jax.ShapeDtypeStruct((SEQ, LANES), jnp.float32)),
        grid=grid,
        in_specs=[pl.BlockSpec((PROJ_TM, D_MODEL), lambda i, j: (i, 0)),
                  pl.BlockSpec((1, D_MODEL), lambda i, j: (0, 0)),
                  pl.BlockSpec((D_MODEL, PROJ_TN), lambda i, j: (0, j)),
                  pl.BlockSpec((D_MODEL, LANES), lambda i, j: (0, 0))],
        out_specs=(pl.BlockSpec((PROJ_TM, PROJ_TN), lambda i, j: (i, j)),
                   pl.BlockSpec((PROJ_TM, LANES), lambda i, j: (i, 0))),
        scratch_shapes=[pltpu.VMEM((PROJ_TM, D_MODEL), jnp.bfloat16)],
        compiler_params=_cparams(("arbitrary", "arbitrary")),
        name="in_proj",
    )(x, g, w_main, w_gate)


def _conv_kernel(prev_ref, main_ref, next_ref, w_ref, b_ref, *out_refs, scale, transposed):
    i = pl.program_id(0)
    n = pl.num_programs(0)
    prev = jnp.where(i > 0, prev_ref[...], 0.0)
    nxt = jnp.where(i < n - 1, next_ref[...], 0.0)
    ext = jnp.concatenate([prev, main_ref[...], nxt], axis=0)
    rows = CONV_R + 2 * SUBLANES
    w = w_ref[...]
    acc = jnp.zeros((CONV_R, CONV_C), jnp.float32) + b_ref[...]
    for j in range(CONV_W):
        shift = (CONV_W // 2 - j) % rows
        xs = ext if shift == 0 else pltpu.roll(ext, shift, axis=0)
        acc = acc + xs[SUBLANES:SUBLANES + CONV_R, :] * w[j:j + 1, :]
    y = acc * _sigmoid(acc) * scale
    out_refs[0][...] = y.astype(jnp.bfloat16)
    if transposed:
        out_refs[1][...] = y.T.astype(jnp.bfloat16)


def _conv(z, conv_w, conv_b, col_block0, scale, transposed):
    nr, ncb = SEQ // CONV_R, A_WIDTH // CONV_C
    rb = CONV_R // SUBLANES
    last = SEQ // SUBLANES - 1
    out_shape = [jax.ShapeDtypeStruct((SEQ, A_WIDTH), jnp.bfloat16)]
    out_specs = [pl.BlockSpec((CONV_R, CONV_C), lambda i, c: (i, c))]
    if transposed:
        out_shape.append(jax.ShapeDtypeStruct((A_WIDTH, SEQ), jnp.bfloat16))
        out_specs.append(pl.BlockSpec((CONV_C, CONV_R), lambda i, c: (c, i)))
    return pl.pallas_call(
        functools.partial(_conv_kernel, scale=scale, transposed=transposed),
        out_shape=tuple(out_shape),
        grid=(nr, ncb),
        in_specs=[pl.BlockSpec((SUBLANES, CONV_C), lambda i, c: (jnp.maximum(i * rb - 1, 0), col_block0 + c)),
                  pl.BlockSpec((CONV_R, CONV_C), lambda i, c: (i, col_block0 + c)),
                  pl.BlockSpec((SUBLANES, CONV_C), lambda i, c: (jnp.minimum((i + 1) * rb, last), col_block0 + c)),
                  pl.BlockSpec((CONV_W, CONV_C), lambda i, c: (0, col_block0 + c)),
                  pl.BlockSpec((1, CONV_C), lambda i, c: (0, col_block0 + c))],
        out_specs=tuple(out_specs),
        compiler_params=_cparams(("arbitrary", "arbitrary")),
        name="conv_k" if transposed else "conv_q",
    )(z, z, z, conv_w, conv_b)


def _log_sigmoid(x):
    return jnp.minimum(x, 0.0) - jnp.log(1.0 + jnp.exp(-jnp.abs(x)))


def _cumsum_rows(x):
    n = x.shape[0]
    row = lax.broadcasted_iota(jnp.int32, x.shape, 0)
    s = 1
    while s < n:
        x = x + jnp.where(row >= s, pltpu.roll(x, s, axis=0), 0.0)
        s *= 2
    return x


def _mlstm_gate_tables(g_ref, bias_ref, backward):
    L = A_CHUNK
    pre = g_ref[...] + bias_ref[...]
    logf = _log_sigmoid(pre)
    pref = _cumsum_rows(logf)
    total = pref[L - 1:L, :]
    cum = (total - pref + logf) if backward else pref
    lane = lax.broadcasted_iota(jnp.int32, (L, LANES), 1)
    f0 = (3 if backward else 1) * A_HEADS
    tab = jnp.where((lane >= f0) & (lane < f0 + A_HEADS), cum, pre)
    return tab, tab.T, total


def _mlstm_chain(q_ref, k_ref, kt_ref, v_ref, tables, h_ref, c_ref, n_ref, m_ref, h, backward):
    L, dh = A_CHUNK, A_HEAD_DIM
    tab, tab_t, total = tables
    ci = (2 if backward else 0) * A_HEADS + h
    cf = (3 if backward else 1) * A_HEADS + h
    i_c, b_c = tab[:, ci:ci + 1], tab[:, cf:cf + 1]
    i_r, b_r = tab_t[ci:ci + 1, :], tab_t[cf:cf + 1, :]
    b_last = total[:, cf:cf + 1]
    hs = slice(h * dh, (h + 1) * dh)
    q_ref, k_ref, v_ref, h_ref = q_ref.at[:, hs], k_ref.at[:, hs], v_ref.at[:, hs], h_ref.at[:, hs]
    kt_ref, c_ref, n_ref, m_ref = kt_ref.at[hs, :], c_ref.at[h], n_ref.at[h], m_ref.at[h]
    q = q_ref[...]
    k = k_ref[...]
    v = v_ref[...].astype(jnp.bfloat16)
    t_idx = lax.broadcasted_iota(jnp.int32, (L, L), 0)
    s_idx = lax.broadcasted_iota(jnp.int32, (L, L), 1)
    visible = (s_idx >= t_idx) if backward else (s_idx <= t_idx)
    d = jnp.where(visible, b_c - b_r + i_r, -jnp.inf)
    m_prev = m_ref[0:1, 0:1]
    m_inter = b_c + m_prev
    m_t = jnp.maximum(m_inter, jnp.max(d, axis=1, keepdims=True))
    w = jnp.exp(d - m_t) * _dot_nt(q, k)
    s_inter = jnp.exp(m_inter - m_t)
    c_old = c_ref[...]
    n_old = n_ref[0:1, :]
    num = _dot(w.astype(jnp.bfloat16), v) + s_inter * _dot(q, c_old.astype(jnp.bfloat16))
    qn = jnp.sum(q.astype(jnp.float32) * n_old, axis=1, keepdims=True)
    den = jnp.sum(w, axis=1, keepdims=True) + s_inter * qn
    h_ref[...] = num / jnp.maximum(jnp.abs(den), jnp.exp(-m_t))

    g_c = b_last - b_c + i_c
    g_r = b_last - b_r + i_r
    m_new = jnp.maximum(b_last + m_prev, jnp.max(g_r, axis=1, keepdims=True))
    wk_c = jnp.exp(g_c - m_new)
    wk_r = jnp.exp(g_r - m_new)
    decay = jnp.exp(b_last + m_prev - m_new)
    ktw = (kt_ref[...].astype(jnp.float32) * wk_r).astype(jnp.bfloat16)
    c_ref[...] = decay * c_old + _dot(ktw, v)
    n_new = decay * n_old + jnp.sum(k.astype(jnp.float32) * wk_c, axis=0, keepdims=True)
    n_ref[...] = jnp.broadcast_to(n_new, n_ref.shape)
    m_ref[...] = jnp.broadcast_to(m_new, m_ref.shape)


def _mlstm_kernel(qf, kf, ktf, vf, gf, qb, kb, ktb, vb, gb, bias_ref, hf_ref, hb_ref,
                  cf_ref, nf_ref, mf_ref, cb_ref, nb_ref, mb_ref):
    @pl.when(pl.program_id(0) == 0)
    def _():
        for r in (cf_ref, nf_ref, mf_ref, cb_ref, nb_ref, mb_ref):
            r[...] = jnp.zeros(r.shape, r.dtype)

    tf = _mlstm_gate_tables(gf, bias_ref, False)
    tb = _mlstm_gate_tables(gb, bias_ref, True)
    for h in range(A_HEADS):
        _mlstm_chain(qf, kf, ktf, vf, tf, hf_ref, cf_ref, nf_ref, mf_ref, h, False)
        _mlstm_chain(qb, kb, ktb, vb, tb, hb_ref, cb_ref, nb_ref, mb_ref, h, True)


def _mlstm(qc, kc, kct, z, gates, bias):
    L, dh = A_CHUNK, A_HEAD_DIM
    nc = SEQ // L

    def specs(chunk):
        return [pl.BlockSpec((L, A_WIDTH), lambda j: (chunk(j), 0)),
                pl.BlockSpec((L, A_WIDTH), lambda j: (chunk(j), 0)),
                pl.BlockSpec((A_WIDTH, L), lambda j: (0, chunk(j))),
                pl.BlockSpec((L, A_WIDTH), lambda j: (chunk(j), Z_VA // A_WIDTH)),
                pl.BlockSpec((L, LANES), lambda j: (chunk(j), 0))]

    fwd = lambda j: j
    bwd = lambda j: nc - 1 - j
    state = [pltpu.VMEM((A_HEADS, dh, dh), jnp.float32), pltpu.VMEM((A_HEADS, SUBLANES, dh), jnp.float32),
             pltpu.VMEM((A_HEADS, SUBLANES, LANES), jnp.float32)]
    return pl.pallas_call(
        _mlstm_kernel,
        out_shape=(jax.ShapeDtypeStruct((SEQ, A_WIDTH), jnp.float32),
                   jax.ShapeDtypeStruct((SEQ, A_WIDTH), jnp.float32)),
        grid=(nc,),
        in_specs=specs(fwd) + specs(bwd) + [pl.BlockSpec((1, LANES), lambda j: (0, 0))],
        out_specs=(pl.BlockSpec((L, A_WIDTH), lambda j: (fwd(j), 0)),
                   pl.BlockSpec((L, A_WIDTH), lambda j: (bwd(j), 0))),
        scratch_shapes=state + state,
        compiler_params=_cparams(("arbitrary",)),
        name="mlstm",
    )(qc, kc, kct, z, gates, qc, kc, kct, z, gates, bias)


def _attn_prep_kernel(q_ref, k_ref, v_ref, qg_ref, kg_ref, invf_ref, qo_ref, ko_ref, vo_ref):
    tm = q_ref.shape[0]
    t = pl.program_id(0) * tm + lax.broadcasted_iota(jnp.int32, (tm, LANES), 0)
    lane = lax.broadcasted_iota(jnp.int32, (tm, LANES), 1)
    half = B_HEAD_DIM // 2
    grid_shift = GRID_W.bit_length() - 1
    pos = jnp.where(lane < half, t >> grid_shift, t & (GRID_W - 1)).astype(jnp.float32)
    ang = pos * invf_ref[...]
    cos = jnp.cos(ang)
    first = (lane & (half - 1)) < (half // 2)
    sin = jnp.where(first, -jnp.sin(ang), jnp.sin(ang))

    def norm_rope(x, g, scale):
        y = x * lax.rsqrt(jnp.mean(x * x, axis=-1, keepdims=True) + NORM_EPS) * g
        partner = jnp.where(first, pltpu.roll(y, LANES - half // 2, axis=1), pltpu.roll(y, half // 2, axis=1))
        return ((y * cos + partner * sin) * scale).astype(jnp.bfloat16)

    for hd in range(B_HEADS):
        sl = slice(hd * B_HEAD_DIM, (hd + 1) * B_HEAD_DIM)
        qo_ref[:, sl] = norm_rope(q_ref[:, sl], qg_ref[...], B_HEAD_DIM ** -0.5 * LOG2_E)
    ones_col = jnp.where(lane == 0, 1.0, 0.0).astype(jnp.bfloat16)
    for hd in range(B_KV_HEADS):
        sl = slice(hd * B_HEAD_DIM, (hd + 1) * B_HEAD_DIM)
        ko_ref[:, sl] = norm_rope(k_ref[:, sl], kg_ref[...], 1.0)
        vo_ref[:, 2 * hd * B_HEAD_DIM:(2 * hd + 1) * B_HEAD_DIM] = v_ref[:, sl].astype(jnp.bfloat16)
        vo_ref[:, (2 * hd + 1) * B_HEAD_DIM:(2 * hd + 2) * B_HEAD_DIM] = ones_col


def _attn_prep(z, q_norm_g, k_norm_g, inv_freq):
    tm = PREP_TM
    return pl.pallas_call(
        _attn_prep_kernel,
        out_shape=(jax.ShapeDtypeStruct((SEQ, B_WIDTH), jnp.bfloat16),
                   jax.ShapeDtypeStruct((SEQ, B_KV_WIDTH), jnp.bfloat16),
                   jax.ShapeDtypeStruct((SEQ, 2 * B_KV_WIDTH), jnp.bfloat16)),
        grid=(SEQ // tm,),
        in_specs=[pl.BlockSpec((tm, B_WIDTH), lambda i: (i, Z_QB // B_WIDTH)),
                  pl.BlockSpec((tm, B_KV_WIDTH), lambda i: (i, Z_KB // B_KV_WIDTH)),
                  pl.BlockSpec((tm, B_KV_WIDTH), lambda i: (i, Z_VB // B_KV_WIDTH)),
                  pl.BlockSpec((1, LANES), lambda i: (0, 0)),
                  pl.BlockSpec((1, LANES), lambda i: (0, 0)),
                  pl.BlockSpec((1, LANES), lambda i: (0, 0))],
        out_specs=(pl.BlockSpec((tm, B_WIDTH), lambda i: (i, 0)),
                   pl.BlockSpec((tm, B_KV_WIDTH), lambda i: (i, 0)),
                   pl.BlockSpec((tm, 2 * B_KV_WIDTH), lambda i: (i, 0))),
        compiler_params=_cparams(("arbitrary",)),
        name="attn_prep",
    )(z, z, z, q_norm_g, k_norm_g, inv_freq)


def _attn_kernel(q_ref, k_ref, v_ref, o_ref, s_ref, m_ref, acc_ref):
    group = B_HEADS // B_KV_HEADS
    tiles = ATT_TK // LANES
    q = jnp.concatenate([q_ref[:, g * B_HEAD_DIM:(g + 1) * B_HEAD_DIM] for g in range(group)], axis=0)
    m_ref[...] = jnp.full(m_ref.shape, -jnp.inf, jnp.float32)
    acc_ref[...] = jnp.zeros(acc_ref.shape, jnp.float32)

    def scores(c, carry):
        off = pl.multiple_of(c * ATT_TK, ATT_TK)
        s = _dot_nt(q, k_ref[pl.ds(off, ATT_TK), :])
        s_ref[:, pl.ds(off, ATT_TK)] = s
        part = s[:, :LANES]
        for j in range(1, tiles):
            part = jnp.maximum(part, s[:, j * LANES:(j + 1) * LANES])
        m_ref[...] = jnp.maximum(m_ref[...], part)
        return carry

    lax.fori_loop(0, SEQ // ATT_TK, scores, 0, unroll=8)
    row_max = jnp.max(m_ref[...], axis=1, keepdims=True)
    m_ref[...] = jnp.broadcast_to(row_max, m_ref.shape)

    def weighted(c, carry):
        off = pl.multiple_of(c * ATT_TK, ATT_TK)
        m = jnp.concatenate([m_ref[...]] * tiles, axis=1)
        p = jnp.exp2(s_ref[:, pl.ds(off, ATT_TK)] - m).astype(jnp.bfloat16)
        acc_ref[...] += _dot(p, v_ref[pl.ds(off, ATT_TK), :])
        return carry

    lax.fori_loop(0, SEQ // ATT_TK, weighted, 0, unroll=8)
    out = acc_ref[:, :B_HEAD_DIM] / acc_ref[:, B_HEAD_DIM:B_HEAD_DIM + 1]
    for g in range(group):
        o_ref[:, g * B_HEAD_DIM:(g + 1) * B_HEAD_DIM] = out[g * ATT_TQ:(g + 1) * ATT_TQ].astype(o_ref.dtype)


def _attn(qr, kr, vr):
    group = B_HEADS // B_KV_HEADS
    rows = group * ATT_TQ
    return pl.pallas_call(
        _attn_kernel,
        out_shape=jax.ShapeDtypeStruct((SEQ, B_WIDTH), jnp.bfloat16),
        grid=(B_KV_HEADS, SEQ // ATT_TQ),
        in_specs=[pl.BlockSpec((ATT_TQ, group * B_HEAD_DIM), lambda kv, i: (i, kv)),
                  pl.BlockSpec((SEQ, B_HEAD_DIM), lambda kv, i: (0, kv)),
                  pl.BlockSpec((SEQ, 2 * B_HEAD_DIM), lambda kv, i: (0, kv))],
        out_specs=pl.BlockSpec((ATT_TQ, group * B_HEAD_DIM), lambda kv, i: (i, kv)),
        scratch_shapes=[pltpu.VMEM((rows, SEQ), jnp.float32), pltpu.VMEM((rows, LANES), jnp.float32),
                        pltpu.VMEM((rows, 2 * B_HEAD_DIM), jnp.float32)],
        compiler_params=_cparams(("arbitrary", "arbitrary")),
        name="attn",
    )(qr, kr, vr)


def _mix_kernel(hf_ref, hb_ref, oa_ref, ng_ref, yb_ref, ga_ref, gb_ref, pa_ref, pb_ref, o_ref):
    parts = []
    for hd in range(A_HEADS):
        sl = slice(hd * A_HEAD_DIM, (hd + 1) * A_HEAD_DIM)
        hs = hf_ref[:, sl] + hb_ref[:, sl]
        y = hs * lax.rsqrt(jnp.mean(hs * hs, axis=-1, keepdims=True) + NORM_EPS) * ng_ref[:, sl]
        parts.append((_sigmoid(oa_ref[:, sl]) * y).astype(jnp.bfloat16))
    ya = jnp.concatenate(parts, axis=1)
    ta = _dot(ya, pa_ref[...])
    tb = _dot(yb_ref[...], pb_ref[...])
    o_ref[...] = (_sigmoid(ga_ref[...]) * ta + _sigmoid(gb_ref[...]) * tb).astype(o_ref.dtype)


def _mix(hf, hb, z, norm_g, yb, pa, pb):
    tm = MIX_TM
    return pl.pallas_call(
        _mix_kernel,
        out_shape=jax.ShapeDtypeStruct((SEQ, D_MODEL), jnp.bfloat16),
        grid=(SEQ // tm,),
        in_specs=[pl.BlockSpec((tm, A_WIDTH), lambda i: (i, 0)),
                  pl.BlockSpec((tm, A_WIDTH), lambda i: (i, 0)),
                  pl.BlockSpec((tm, A_WIDTH), lambda i: (i, Z_OA // A_WIDTH)),
                  pl.BlockSpec((1, A_WIDTH), lambda i: (0, 0)),
                  pl.BlockSpec((tm, B_WIDTH), lambda i: (i, 0)),
                  pl.BlockSpec((tm, D_MODEL), lambda i: (i, Z_GA // D_MODEL)),
                  pl.BlockSpec((tm, D_MODEL), lambda i: (i, Z_GB // D_MODEL)),
                  pl.BlockSpec((A_WIDTH, D_MODEL), lambda i: (0, 0)),
                  pl.BlockSpec((B_WIDTH, D_MODEL), lambda i: (0, 0))],
        out_specs=pl.BlockSpec((tm, D_MODEL), lambda i: (i, 0)),
        compiler_params=_cparams(("arbitrary",)),
        name="mix",
    )(hf, hb, z, norm_g, yb, z, z, pa, pb)


def _split_bf16(a):
    hi = a.astype(jnp.bfloat16)
    lo = (a - hi.astype(jnp.float32)).astype(jnp.bfloat16)
    return hi, lo


def _out_proj_kernel(mx_ref, w_ref, x_ref, g_ref, wr_ref, br_ref, h1_ref, xn_ref, lg_ref):
    h1 = x_ref[...] + _dot(mx_ref[...], w_ref[...])
    h1_ref[...] = h1
    xn = h1 * lax.rsqrt(jnp.mean(h1 * h1, axis=-1, keepdims=True) + NORM_EPS) * g_ref[...]
    xn_ref[...] = xn
    x_hi, x_lo = _split_bf16(xn)
    w_hi, w_lo = _split_bf16(wr_ref[...])
    lg_ref[...] = _dot_nt(w_hi, x_hi) + (_dot_nt(w_hi, x_lo) + _dot_nt(w_lo, x_hi)) + br_ref[...]


def _out_proj(mixed, w_out, x, g, w_router_t, b_router):
    tm = OUT_TM
    return pl.pallas_call(
        _out_proj_kernel,
        out_shape=(jax.ShapeDtypeStruct((SEQ, D_MODEL), jnp.float32),
                   jax.ShapeDtypeStruct((SEQ, D_MODEL), jnp.float32),
                   jax.ShapeDtypeStruct((N_EXPERTS, SEQ), jnp.float32)),
        grid=(SEQ // tm,),
        in_specs=[pl.BlockSpec((tm, D_MODEL), lambda i: (i, 0)),
                  pl.BlockSpec((D_MODEL, D_MODEL), lambda i: (0, 0)),
                  pl.BlockSpec((tm, D_MODEL), lambda i: (i, 0)),
                  pl.BlockSpec((1, D_MODEL), lambda i: (0, 0)),
                  pl.BlockSpec((N_EXPERTS, D_MODEL), lambda i: (0, 0)),
                  pl.BlockSpec((N_EXPERTS, 1), lambda i: (0, 0))],
        out_specs=(pl.BlockSpec((tm, D_MODEL), lambda i: (i, 0)),
                   pl.BlockSpec((tm, D_MODEL), lambda i: (i, 0)),
                   pl.BlockSpec((N_EXPERTS, tm), lambda i: (0, i))),
        compiler_params=_cparams(("arbitrary",)),
        name="out_proj",
    )(mixed, w_out, x, g, w_router_t, b_router)


def _route_kernel(lg_ref, e_ref, w_ref, pos_ref, cnt_ref, r_ref, carry_ref):
    tc = ROUTE_TC
    carry_ref[...] = jnp.zeros(carry_ref.shape, jnp.float32)
    eidx = lax.broadcasted_iota(jnp.int32, (N_EXPERTS, tc), 0)
    tri_r = lax.broadcasted_iota(jnp.int32, (tc, tc), 0)
    tri_c = lax.broadcasted_iota(jnp.int32, (tc, tc), 1)
    before = jnp.where(tri_r < tri_c, 1.0, 0.0).astype(jnp.bfloat16)

    def body(c, carry):
        off = pl.multiple_of(c * tc, tc)
        lg = lg_ref[:, pl.ds(off, tc)]
        vals, sels = [], []
        for _ in range(TOP_K):
            mx = jnp.max(lg, axis=0, keepdims=True)
            idx = jnp.min(jnp.where(lg == mx, eidx, N_EXPERTS), axis=0, keepdims=True)
            sel = eidx == idx
            vals.append(mx)
            sels.append(sel)
            lg = jnp.where(sel, -jnp.inf, lg)
        ex = [jnp.exp(v - vals[0]) for v in vals]
        tot = ex[0] + ex[1] + ex[2] + ex[3]
        chosen = jnp.where(sels[0] | sels[1] | sels[2] | sels[3], 1.0, 0.0)
        rank = _dot(chosen.astype(jnp.bfloat16), before) + carry_ref[:, 0:1]
        carry_ref[...] = carry_ref[...] + jnp.sum(chosen, axis=1, keepdims=True)
        for k in range(TOP_K):
            e_ref[k:k + 1, pl.ds(off, tc)] = jnp.sum(jnp.where(sels[k], eidx, 0), axis=0, keepdims=True)
            w_ref[k:k + 1, pl.ds(off, tc)] = ex[k] / tot
            r_ref[k:k + 1, pl.ds(off, tc)] = jnp.sum(jnp.where(sels[k], rank, 0.0), axis=0, keepdims=True)
        return carry

    lax.fori_loop(0, SEQ // tc, body, 0)

    counts = carry_ref[...].astype(jnp.int32)
    cnt_ref[...] = counts
    tile_shift = MOE_TM.bit_length() - 1
    padded = (((counts + (MOE_TM - 1)) >> tile_shift) << tile_shift).astype(jnp.float32)
    start = (_cumsum_rows(padded) - padded)[:, 0:1]

    def place(c, carry):
        off = pl.multiple_of(c * tc, tc)
        for k in range(TOP_K):
            sel = eidx == e_ref[k:k + 1, pl.ds(off, tc)]
            base = jnp.sum(jnp.where(sel, start, 0.0), axis=0, keepdims=True)
            pos_ref[k:k + 1, pl.ds(off, tc)] = (base + r_ref[k:k + 1, pl.ds(off, tc)]).astype(jnp.int32)
        return carry

    lax.fori_loop(0, SEQ // tc, place, 0)


def _route(logits_t):
    return pl.pallas_call(
        _route_kernel,
        out_shape=(jax.ShapeDtypeStruct((TOP_K, SEQ), jnp.int32),
                   jax.ShapeDtypeStruct((TOP_K, SEQ), jnp.float32),
                   jax.ShapeDtypeStruct((TOP_K, SEQ), jnp.int32),
                   jax.ShapeDtypeStruct((N_EXPERTS, LANES), jnp.int32)),
        scratch_shapes=[pltpu.VMEM((TOP_K, SEQ), jnp.float32), pltpu.VMEM((N_EXPERTS, LANES), jnp.float32)],
        compiler_params=_cparams(None),
        name="route",
    )(logits_t)


def _dispatch_kernel(pos_ref, xw_ref, xg_ref, sem):
    base = pl.program_id(0) * DISP_TT

    def copy(j, k):
        return pltpu.make_async_copy(xw_ref.at[pl.ds(j, 1)], xg_ref.at[pl.ds(pos_ref[k * SEQ + base + j], 1)], sem)

    def start(j, c):
        for k in range(TOP_K):
            copy(j, k).start()
        return c

    def wait(j, c):
        for k in range(TOP_K):
            copy(j, k).wait()
        return c

    lax.fori_loop(0, DISP_TT, start, 0, unroll=8)
    lax.fori_loop(0, DISP_TT, wait, 0, unroll=8)


def _dispatch(pos_flat, xw):
    return pl.pallas_call(
        _dispatch_kernel,
        out_shape=jax.ShapeDtypeStruct((MOE_ROWS, D_MODEL), jnp.float32),
        grid_spec=pltpu.PrefetchScalarGridSpec(
            num_scalar_prefetch=1,
            grid=(SEQ // DISP_TT,),
            in_specs=[pl.BlockSpec((DISP_TT, D_MODEL), lambda i, pos: (i, 0))],
            out_specs=pl.BlockSpec(memory_space=pl.ANY),
            scratch_shapes=[pltpu.SemaphoreType.DMA(())]),
        compiler_params=_cparams(("arbitrary",)),
        name="dispatch",
    )(pos_flat, xw)


def _experts_kernel(te_ref, nv_ref, na_ref, xg_ref, wg_ref, wu_ref, bg_ref, bu_ref, wd_ref, bd_ref, y_ref, x_ref):
    t = pl.program_id(0)
    f = pl.program_id(1)

    @pl.when(t < na_ref[0])
    def _():
        valid = nv_ref[t]

        @pl.when(f == 0)
        def _():
            row = lax.broadcasted_iota(jnp.int32, xg_ref.shape, 0)
            x_ref[...] = jnp.where(row < valid, xg_ref[...], 0.0).astype(jnp.bfloat16)
            y_ref[...] = jnp.zeros(y_ref.shape, jnp.float32) + bd_ref[...]

        def ffn(rows):
            x = x_ref[:rows, :]
            gate = _dot(x, wg_ref[...].astype(jnp.bfloat16)) + bg_ref[...]
            up = _dot(x, wu_ref[...].astype(jnp.bfloat16)) + bu_ref[...]
            gate = jnp.minimum(gate, SWIGLU_LIMIT)
            up = jnp.clip(up, -SWIGLU_LIMIT, SWIGLU_LIMIT)
            act = (up + 1.0) * (gate * _sigmoid(gate * SWIGLU_ALPHA))
            y_ref[:rows, :] += _dot(act.astype(jnp.bfloat16), wd_ref[...].astype(jnp.bfloat16))

        @pl.when(valid > MOE_TM // 2)
        def _():
            ffn(MOE_TM)

        @pl.when(valid <= MOE_TM // 2)
        def _():
            ffn(MOE_TM // 2)


def _experts(tile_expert, tile_valid, n_active, xg, w_gate_up, b_gate_up, w_down, b_down):
    nf = D_FF // MOE_TF
    tm, tf = MOE_TM, MOE_TF

    def tile(t, na):
        return jnp.minimum(t, na[0] - 1)

    def fcol(t, f, na):
        return jnp.where(t < na[0], f, nf - 1)

    def rows(t, f, te, nv, na):
        return (tile(t, na), 0)

    def weight(row_block, col_block):
        def index_map(t, f, te, nv, na):
            fc = fcol(t, f, na)
            return (0, te[tile(t, na)], row_block(fc), col_block(fc))
        return index_map

    zero = lambda fc: 0
    same = lambda fc: fc
    upper = lambda fc: nf + fc
    return pl.pallas_call(
        _experts_kernel,
        out_shape=jax.ShapeDtypeStruct((MOE_ROWS, D_MODEL), jnp.float32),
        grid_spec=pltpu.PrefetchScalarGridSpec(
            num_scalar_prefetch=3,
            grid=(MOE_TILES, nf),
            in_specs=[
                pl.BlockSpec((tm, D_MODEL), rows),
                pl.BlockSpec((None, None, D_MODEL, tf), weight(zero, same)),
                pl.BlockSpec((None, None, D_MODEL, tf), weight(zero, upper)),
                pl.BlockSpec((None, None, 1, tf), weight(zero, same)),
                pl.BlockSpec((None, None, 1, tf), weight(zero, upper)),
                pl.BlockSpec((None, None, tf, D_MODEL), weight(same, zero)),
                pl.BlockSpec((None, None, 1, D_MODEL), weight(zero, zero)),
            ],
            out_specs=pl.BlockSpec((tm, D_MODEL), rows),
            scratch_shapes=[pltpu.VMEM((tm, D_MODEL), jnp.bfloat16)]),
        compiler_params=_cparams(("arbitrary", "arbitrary")),
        name="experts",
    )(tile_expert, tile_valid, n_active, xg, w_gate_up, w_gate_up, b_gate_up, b_gate_up, w_down, b_down)


def _combine_kernel(pos_ref, yg_ref, w_ref, h1_ref, g_ref, o_ref, buf_ref, sem):
    tt = COMB_TT
    base = pl.program_id(0) * tt

    def copy(j, k):
        return pltpu.make_async_copy(yg_ref.at[pl.ds(pos_ref[k * SEQ + base + j], 1)],
                                     buf_ref.at[k, pl.ds(j, 1)], sem)

    def start(j, c):
        for k in range(TOP_K):
            copy(j, k).start()
        return c

    def wait(j, c):
        for k in range(TOP_K):
            copy(j, k).wait()
        return c

    lax.fori_loop(0, tt, start, 0, unroll=8)
    lax.fori_loop(0, tt, wait, 0, unroll=8)
    w = w_ref[...]
    h2 = h1_ref[...]
    for k in range(TOP_K):
        h2 = h2 + w[:, k:k + 1] * buf_ref[k]
    o_ref[...] = h2 * lax.rsqrt(jnp.mean(h2 * h2, axis=-1, keepdims=True) + NORM_EPS) * g_ref[...]


def _combine(pos_flat, yg, gate_w, h1, g):
    tt = COMB_TT
    return pl.pallas_call(
        _combine_kernel,
        out_shape=jax.ShapeDtypeStruct((SEQ, D_MODEL), jnp.float32),
        grid_spec=pltpu.PrefetchScalarGridSpec(
            num_scalar_prefetch=1,
            grid=(SEQ // tt,),
            in_specs=[pl.BlockSpec(memory_space=pl.ANY),
                      pl.BlockSpec((tt, TOP_K), lambda i, pos: (i, 0)),
                      pl.BlockSpec((tt, D_MODEL), lambda i, pos: (i, 0)),
                      pl.BlockSpec((1, D_MODEL), lambda i, pos: (0, 0))],
            out_specs=pl.BlockSpec((tt, D_MODEL), lambda i, pos: (i, 0)),
            scratch_shapes=[pltpu.VMEM((TOP_K, tt, D_MODEL), jnp.float32), pltpu.SemaphoreType.DMA(())]),
        compiler_params=_cparams(("arbitrary",)),
        name="combine",
    )(pos_flat, yg, gate_w, h1, g)


def _regroup_w_in(w):
    qa_oa = w[:, :4 * A_WIDTH]
    g0 = 4 * A_WIDTH
    gates = w[:, g0:g0 + 4 * A_HEADS]
    b0 = g0 + 4 * A_HEADS
    qb = w[:, b0:b0 + B_WIDTH]
    kv = w[:, b0 + B_WIDTH:b0 + B_WIDTH + 2 * B_KV_WIDTH]
    gm = w[:, b0 + B_WIDTH + 2 * B_KV_WIDTH:]
    main = jnp.concatenate([qa_oa, gm, qb, kv], axis=1).astype(jnp.bfloat16)
    gates = jnp.pad(gates, ((0, 0), (0, LANES - 4 * A_HEADS))).astype(jnp.bfloat16)
    return main, gates


def kernel(x, norm_mix_g, w_in, conv_w, conv_b, b_mlstm_gates, mlstm_norm_g, q_norm_g, k_norm_g, w_proj_a,
           w_proj_b, w_out, norm_ffn_g, w_router, b_router, w_gate_up, b_gate_up, w_down, b_down, norm_final_g):
    assert x.shape == (1, SEQ, D_MODEL) and w_in.shape[0] == 1
    x2 = x[0]
    w_main, w_gates = _regroup_w_in(w_in[0])
    z, gates = _in_proj(x2, norm_mix_g, w_main, w_gates)

    (qc,) = _conv(z, conv_w[0], conv_b, 0, A_HEAD_DIM ** -0.5, False)
    kc, kct = _conv(z, conv_w[0], conv_b, A_WIDTH // CONV_C, 1.0, True)
    gate_bias = jnp.pad(b_mlstm_gates.reshape(1, 4 * A_HEADS), ((0, 0), (0, LANES - 4 * A_HEADS)))
    hf, hb = _mlstm(qc, kc, kct, z, gates, gate_bias)

    half = B_HEAD_DIM // 2
    inv_freq = ROPE_THETA ** (-jnp.arange(0, half, 2, dtype=jnp.float32) / half)
    qr, kr, vr = _attn_prep(z, q_norm_g, k_norm_g, jnp.tile(inv_freq, 4).reshape(1, LANES))
    yb = _attn(qr, kr, vr)

    mixed = _mix(hf, hb, z, mlstm_norm_g, yb, w_proj_a[0].astype(jnp.bfloat16), w_proj_b[0].astype(jnp.bfloat16))
    h1, xw, logits_t = _out_proj(mixed, w_out[0].astype(jnp.bfloat16), x2, norm_ffn_g, w_router[0].T,
                                 b_router.reshape(N_EXPERTS, 1))

    _, gate_w, pos, counts = _route(logits_t)
    pos = pos.reshape(-1)
    counts = counts[:, 0]
    padded = ((counts + MOE_TM - 1) // MOE_TM) * MOE_TM
    pad_end = jnp.cumsum(padded)
    pad_start = pad_end - padded
    n_active = (pad_end[-1] // MOE_TM).astype(jnp.int32).reshape(1)
    tile_start = jnp.arange(MOE_TILES, dtype=jnp.int32) * MOE_TM
    tile_expert = jnp.minimum(jnp.sum(tile_start[:, None] >= pad_end[None, :], axis=1), N_EXPERTS - 1)
    tile_expert = tile_expert.astype(jnp.int32)
    tile_valid = jnp.clip(counts[tile_expert] - (tile_start - pad_start[tile_expert]), 0, MOE_TM).astype(jnp.int32)

    xg = _dispatch(pos, xw)
    yg = _experts(tile_expert, tile_valid, n_active, xg, w_gate_up, b_gate_up.reshape(1, N_EXPERTS, 1, 2 * D_FF),
                  w_down, b_down.reshape(1, N_EXPERTS, 1, D_MODEL))
    out = _combine(pos, yg, gate_w.T, h1, norm_final_g.reshape(1, D_MODEL))
    return out[None]
```

```python
import functools

import jax
import jax.numpy as jnp
import numpy as np
from jax import lax
from jax.experimental import pallas as pl
from jax.experimental.pallas import tpu as pltpu

D_MODEL = 2048
SEQ = 8192
GRID_W = 64
NORM_EPS = 1e-6
A_HEADS = 4
A_HEAD_DIM = 256
A_WIDTH = A_HEADS * A_HEAD_DIM
CONV_W = 5
B_HEADS = 8
B_KV_HEADS = 2
B_HEAD_DIM = 128
B_WIDTH = B_HEADS * B_HEAD_DIM
B_KV_WIDTH = B_KV_HEADS * B_HEAD_DIM
ROPE_THETA = 10000.0
N_EXPERTS = 32
TOP_K = 4
D_FF = D_MODEL
SWIGLU_LIMIT = 7.0
SWIGLU_ALPHA = 1.702
LOG2_E = 1.4426950408889634

LANES = 128
SUBLANES = 8
VMEM_LIMIT = 56 * 1024 * 1024

Z_QK = 0
Z_VA = 2048
Z_OA = 3072
Z_GA = 4096
Z_GB = 6144
Z_QB = 8192
Z_KB = 9216
Z_VB = 9472
Z_WIDTH = 9728

PROJ_TM, PROJ_TN = 1024, 512
CONV_R, CONV_C = 1024, 256
A_CHUNK = 256
PREP_TM = 512
ATT_TQ, ATT_TK = 128, 512
MIX_TM = 256
OUT_TM = 256
ROUTE_TC = 512
MOE_SUB = 256
MOE_NSUB = 4
MOE_TM = MOE_SUB * MOE_NSUB
MOE_TF = 256
MOE_ROWS = SEQ * TOP_K + N_EXPERTS * MOE_SUB
MOE_TILES = SEQ * TOP_K // MOE_TM + N_EXPERTS
DISP_TT = 512
COMB_TT = 256


def _cparams(sem, vmem=VMEM_LIMIT):
    return pltpu.CompilerParams(dimension_semantics=sem, vmem_limit_bytes=vmem)


def _sigmoid(x):
    return 1.0 / (1.0 + jnp.exp(-x))


def _dot(a, b):
    return jnp.dot(a, b, preferred_element_type=jnp.float32)


def _dot_nt(a, b):
    return lax.dot_general(a, b, (((1,), (1,)), ((), ())), preferred_element_type=jnp.float32)


def _in_proj_kernel(x_ref, g_ref, w_ref, wg_ref, z_ref, gate_ref, u_ref):
    @pl.when(pl.program_id(1) == 0)
    def _():
        x = x_ref[...]
        y = x * lax.rsqrt(jnp.mean(x * x, axis=-1, keepdims=True) + NORM_EPS) * g_ref[...]
        u = y.astype(jnp.bfloat16)
        u_ref[...] = u
        gate_ref[...] = _dot(u, wg_ref[...])

    z_ref[...] = _dot(u_ref[...], w_ref[...])


def _in_proj(x, g, w_main, w_gate):
    grid = (SEQ // PROJ_TM, Z_WIDTH // PROJ_TN)
    return pl.pallas_call(
        _in_proj_kernel,
        out_shape=(jax.ShapeDtypeStruct((SEQ, Z_WIDTH), jnp.float32),
                   jax.ShapeDtypeStruct((SEQ, LANES), jnp.float32)),
        grid=grid,
        in_specs=[pl.BlockSpec((PROJ_TM, D_MODEL), lambda i, j: (i, 0)),
                  pl.BlockSpec((1, D_MODEL), lambda i, j: (0, 0)),
                  pl.BlockSpec((D_MODEL, PROJ_TN), lambda i, j: (0, j)),
                  pl.BlockSpec((D_MODEL, LANES), lambda i, j: (0, 0))],
        out_specs=(pl.BlockSpec((PROJ_TM, PROJ_TN), lambda i, j: (i, j)),
                   pl.BlockSpec((PROJ_TM, LANES), lambda i, j: (i, 0))),
        scratch_shapes=[pltpu.VMEM((PROJ_TM, D_MODEL), jnp.bfloat16)],
        compiler_params=_cparams(("arbitrary", "arbitrary")),
        name="in_proj",
    )(x, g, w_main, w_gate)


def _conv_kernel(prev_ref, main_ref, next_ref, w_ref, b_ref, *out_refs, scale, transposed):
    i = pl.program_id(0)
    n = pl.num_programs(0)
    prev = jnp.where(i > 0, prev_ref[...], 0.0)
    nxt = jnp.where(i < n - 1, next_ref[...], 0.0)
    ext = jnp.concatenate([prev, main_ref[...], nxt], axis=0)
    rows = CONV_R + 2 * SUBLANES
    w = w_ref[...]
    acc = jnp.zeros((CONV_R, CONV_C), jnp.float32) + b_ref[...]
    for j in range(CONV_W):
        shift = (CONV_W // 2 - j) % rows
        xs = ext if shift == 0 else pltpu.roll(ext, shift, axis=0)
        acc = acc + xs[SUBLANES:SUBLANES + CONV_R, :] * w[j:j + 1, :]
    y = acc * _sigmoid(acc) * scale
    out_refs[0][...] = y.astype(jnp.bfloat16)
    if transposed:
        out_refs[1][...] = y.T.astype(jnp.bfloat16)


def _conv(z, conv_w, conv_b, col_block0, scale, transposed):
    nr, ncb = SEQ // CONV_R, A_WIDTH // CONV_C
    rb = CONV_R // SUBLANES
    last = SEQ // SUBLANES - 1
    out_shape = [jax.ShapeDtypeStruct((SEQ, A_WIDTH), jnp.bfloat16)]
    out_specs = [pl.BlockSpec((CONV_R, CONV_C), lambda i, c: (i, c))]
    if transposed:
        out_shape.append(jax.ShapeDtypeStruct((A_WIDTH, SEQ), jnp.bfloat16))
        out_specs.append(pl.BlockSpec((CONV_C, CONV_R), lambda i, c: (c, i)))
    return pl.pallas_call(
        functools.partial(_conv_kernel, scale=scale, transposed=transposed),
        out_shape=tuple(out_shape),
        grid=(nr, ncb),
        in_specs=[pl.BlockSpec((SUBLANES, CONV_C), lambda i, c: (jnp.maximum(i * rb - 1, 0), col_block0 + c)),
                  pl.BlockSpec((CONV_R, CONV_C), lambda i, c: (i, col_block0 + c)),
                  pl.BlockSpec((SUBLANES, CONV_C), lambda i, c: (jnp.minimum((i + 1) * rb, last), col_block0 + c)),
                  pl.BlockSpec((CONV_W, CONV_C), lambda i, c: (0, col_block0 + c)),
                  pl.BlockSpec((1, CONV_C), lambda i, c: (0, col_block0 + c))],
        out_specs=tuple(out_specs),
        compiler_params=_cparams(("arbitrary", "arbitrary")),
        name="conv_k" if transposed else "conv_q",
    )(z, z, z, conv_w, conv_b)


def _log_sigmoid(x):
    return jnp.minimum(x, 0.0) - jnp.log(1.0 + jnp.exp(-jnp.abs(x)))


def _cumsum_rows(x):
    n = x.shape[0]
    row = lax.broadcasted_iota(jnp.int32, x.shape, 0)
    s = 1
    while s < n:
        x = x + jnp.where(row >= s, pltpu.roll(x, s, axis=0), 0.0)
        s *= 2
    return x


def _mlstm_gate_tables(g_ref, bias_ref, backward):
    L = A_CHUNK
    pre = g_ref[...] + bias_ref[...]
    logf = _log_sigmoid(pre)
    pref = _cumsum_rows(logf)
    total = pref[L - 1:L, :]
    cum = (total - pref + logf) if backward else pref
    lane = lax.broadcasted_iota(jnp.int32, (L, LANES), 1)
    f0 = (3 if backward else 1) * A_HEADS
    tab = jnp.where((lane >= f0) & (lane < f0 + A_HEADS), cum, pre)
    return tab, tab.T, total


def _mlstm_chain(q_ref, k_ref, kt_ref, v_ref, tables, h_ref, c_ref, n_ref, m_ref, h, backward):
    L, dh = A_CHUNK, A_HEAD_DIM
    tab, tab_t, total = tables
    ci = (2 if backward else 0) * A_HEADS + h
    cf = (3 if backward else 1) * A_HEADS + h
    i_c, b_c = tab[:, ci:ci + 1], tab[:, cf:cf + 1]
    i_r, b_r = tab_t[ci:ci + 1, :], tab_t[cf:cf + 1, :]
    b_last = total[:, cf:cf + 1]
    hs = slice(h * dh, (h + 1) * dh)
    q_ref, k_ref, v_ref, h_ref = q_ref.at[:, hs], k_ref.at[:, hs], v_ref.at[:, hs], h_ref.at[:, hs]
    kt_ref, c_ref, n_ref, m_ref = kt_ref.at[hs, :], c_ref.at[h], n_ref.at[h], m_ref.at[h]
    q = q_ref[...]
    k = k_ref[...]
    v = v_ref[...].astype(jnp.bfloat16)
    t_idx = lax.broadcasted_iota(jnp.int32, (L, L), 0)
    s_idx = lax.broadcasted_iota(jnp.int32, (L, L), 1)
    visible = (s_idx >= t_idx) if backward else (s_idx <= t_idx)
    d = jnp.where(visible, b_c - b_r + i_r, -jnp.inf)
    m_prev = m_ref[0:1, 0:1]
    m_inter = b_c + m_prev
    m_t = jnp.maximum(m_inter, jnp.max(d, axis=1, keepdims=True))
    w = jnp.exp(d - m_t) * _dot_nt(q, k)
    s_inter = jnp.exp(m_inter - m_t)
    c_old = c_ref[...]
    n_old = n_ref[0:1, :]
    num = _dot(w.astype(jnp.bfloat16), v) + s_inter * _dot(q, c_old.astype(jnp.bfloat16))
    qn = jnp.sum(q.astype(jnp.float32) * n_old, axis=1, keepdims=True)
    den = jnp.sum(w, axis=1, keepdims=True) + s_inter * qn
    h_ref[...] = num / jnp.maximum(jnp.abs(den), jnp.exp(-m_t))

    g_c = b_last - b_c + i_c
    g_r = b_last - b_r + i_r
    m_new = jnp.maximum(b_last + m_prev, jnp.max(g_r, axis=1, keepdims=True))
    wk_c = jnp.exp(g_c - m_new)
    wk_r = jnp.exp(g_r - m_new)
    decay = jnp.exp(b_last + m_prev - m_new)
    ktw = (kt_ref[...].astype(jnp.float32) * wk_r).astype(jnp.bfloat16)
    c_ref[...] = decay * c_old + _dot(ktw, v)
    n_new = decay * n_old + jnp.sum(k.astype(jnp.float32) * wk_c, axis=0, keepdims=True)
    n_ref[...] = jnp.broadcast_to(n_new, n_ref.shape)
    m_ref[...] = jnp.broadcast_to(m_new, m_ref.shape)


def _mlstm_kernel(qf, kf, ktf, vf, gf, qb, kb, ktb, vb, gb, bias_ref, hf_ref, hb_ref,
                  cf_ref, nf_ref, mf_ref, cb_ref, nb_ref, mb_ref):
    @pl.when(pl.program_id(0) == 0)
    def _():
        for r in (cf_ref, nf_ref, mf_ref, cb_ref, nb_ref, mb_ref):
            r[...] = jnp.zeros(r.shape, r.dtype)

    tf = _mlstm_gate_tables(gf, bias_ref, False)
    tb = _mlstm_gate_tables(gb, bias_ref, True)
    for h in range(A_HEADS):
        _mlstm_chain(qf, kf, ktf, vf, tf, hf_ref, cf_ref, nf_ref, mf_ref, h, False)
        _mlstm_chain(qb, kb, ktb, vb, tb, hb_ref, cb_ref, nb_ref, mb_ref, h, True)


def _mlstm(qc, kc, kct, z, gates, bias):
    L, dh = A_CHUNK, A_HEAD_DIM
    nc = SEQ // L

    def specs(chunk):
        return [pl.BlockSpec((L, A_WIDTH), lambda j: (chunk(j), 0)),
                pl.BlockSpec((L, A_WIDTH), lambda j: (chunk(j), 0)),
                pl.BlockSpec((A_WIDTH, L), lambda j: (0, chunk(j))),
                pl.BlockSpec((L, A_WIDTH), lambda j: (chunk(j), Z_VA // A_WIDTH)),
                pl.BlockSpec((L, LANES), lambda j: (chunk(j), 0))]

    fwd = lambda j: j
    bwd = lambda j: nc - 1 - j
    state = [pltpu.VMEM((A_HEADS, dh, dh), jnp.float32), pltpu.VMEM((A_HEADS, SUBLANES, dh), jnp.float32),
             pltpu.VMEM((A_HEADS, SUBLANES, LANES), jnp.float32)]
    return pl.pallas_call(
        _mlstm_kernel,
        out_shape=(jax.ShapeDtypeStruct((SEQ, A_WIDTH), jnp.float32),
                   jax.ShapeDtypeStruct((SEQ, A_WIDTH), jnp.float32)),
        grid=(nc,),
        in_specs=specs(fwd) + specs(bwd) + [pl.BlockSpec((1, LANES), lambda j: (0, 0))],
        out_specs=(pl.BlockSpec((L, A_WIDTH), lambda j: (fwd(j), 0)),
                   pl.BlockSpec((L, A_WIDTH), lambda j: (bwd(j), 0))),
        scratch_shapes=state + state,
        compiler_params=_cparams(("arbitrary",)),
        name="mlstm",
    )(qc, kc, kct, z, gates, qc, kc, kct, z, gates, bias)


def _attn_prep_kernel(q_ref, k_ref, v_ref, qg_ref, kg_ref, invf_ref, qo_ref, ko_ref, vo_ref):
    tm = q_ref.shape[0]
    t = pl.program_id(0) * tm + lax.broadcasted_iota(jnp.int32, (tm, LANES), 0)
    lane = lax.broadcasted_iota(jnp.int32, (tm, LANES), 1)
    half = B_HEAD_DIM // 2
    grid_shift = GRID_W.bit_length() - 1
    pos = jnp.where(lane < half, t >> grid_shift, t & (GRID_W - 1)).astype(jnp.float32)
    ang = pos * invf_ref[...]
    cos = jnp.cos(ang)
    first = (lane & (half - 1)) < (half // 2)
    sin = jnp.where(first, -jnp.sin(ang), jnp.sin(ang))

    def norm_rope(x, g, scale):
        y = x * lax.rsqrt(jnp.mean(x * x, axis=-1, keepdims=True) + NORM_EPS) * g
        partner = jnp.where(first, pltpu.roll(y, LANES - half // 2, axis=1), pltpu.roll(y, half // 2, axis=1))
        return ((y * cos + partner * sin) * scale).astype(jnp.bfloat16)

    for hd in range(B_HEADS):
        sl = slice(hd * B_HEAD_DIM, (hd + 1) * B_HEAD_DIM)
        qo_ref[:, sl] = norm_rope(q_ref[:, sl], qg_ref[...], B_HEAD_DIM ** -0.5 * LOG2_E)
    ones_col = jnp.where(lane == 0, 1.0, 0.0).astype(jnp.bfloat16)
    for hd in range(B_KV_HEADS):
        sl = slice(hd * B_HEAD_DIM, (hd + 1) * B_HEAD_DIM)
        ko_ref[:, sl] = norm_rope(k_ref[:, sl], kg_ref[...], 1.0)
        vo_ref[:, 2 * hd * B_HEAD_DIM:(2 * hd + 1) * B_HEAD_DIM] = v_ref[:, sl].astype(jnp.bfloat16)
        vo_ref[:, (2 * hd + 1) * B_HEAD_DIM:(2 * hd + 2) * B_HEAD_DIM] = ones_col


def _attn_prep(z, q_norm_g, k_norm_g, inv_freq):
    tm = PREP_TM
    return pl.pallas_call(
        _attn_prep_kernel,
        out_shape=(jax.ShapeDtypeStruct((SEQ, B_WIDTH), jnp.bfloat16),
                   jax.ShapeDtypeStruct((SEQ, B_KV_WIDTH), jnp.bfloat16),
                   jax.ShapeDtypeStruct((SEQ, 2 * B_KV_WIDTH), jnp.bfloat16)),
        grid=(SEQ // tm,),
        in_specs=[pl.BlockSpec((tm, B_WIDTH), lambda i: (i, Z_QB // B_WIDTH)),
                  pl.BlockSpec((tm, B_KV_WIDTH), lambda i: (i, Z_KB // B_KV_WIDTH)),
                  pl.BlockSpec((tm, B_KV_WIDTH), lambda i: (i, Z_VB // B_KV_WIDTH)),
                  pl.BlockSpec((1, LANES), lambda i: (0, 0)),
                  pl.BlockSpec((1, LANES), lambda i: (0, 0)),
                  pl.BlockSpec((1, LANES), lambda i: (0, 0))],
        out_specs=(pl.BlockSpec((tm, B_WIDTH), lambda i: (i, 0)),
                   pl.BlockSpec((tm, B_KV_WIDTH), lambda i: (i, 0)),
                   pl.BlockSpec((tm, 2 * B_KV_WIDTH), lambda i: (i, 0))),
        compiler_params=_cparams(("arbitrary",)),
        name="attn_prep",
    )(z, z, z, q_norm_g, k_norm_g, inv_freq)


def _attn_kernel(q_ref, k_ref, v_ref, o_ref, s_ref, m_ref, acc_ref):
    group = B_HEADS // B_KV_HEADS
    tiles = ATT_TK // LANES
    q = jnp.concatenate([q_ref[:, g * B_HEAD_DIM:(g + 1) * B_HEAD_DIM] for g in range(group)], axis=0)
    m_ref[...] = jnp.full(m_ref.shape, -jnp.inf, jnp.float32)
    acc_ref[...] = jnp.zeros(acc_ref.shape, jnp.float32)

    def scores(c, carry):
        off = pl.multiple_of(c * ATT_TK, ATT_TK)
        s = _dot_nt(q, k_ref[pl.ds(off, ATT_TK), :])
        s_ref[:, pl.ds(off, ATT_TK)] = s
        part = s[:, :LANES]
        for j in range(1, tiles):
            part = jnp.maximum(part, s[:, j * LANES:(j + 1) * LANES])
        m_ref[...] = jnp.maximum(m_ref[...], part)
        return carry

    lax.fori_loop(0, SEQ // ATT_TK, scores, 0, unroll=8)
    row_max = jnp.max(m_ref[...], axis=1, keepdims=True)
    m_ref[...] = jnp.broadcast_to(row_max, m_ref.shape)

    def weighted(c, carry):
        off = pl.multiple_of(c * ATT_TK, ATT_TK)
        m = jnp.concatenate([m_ref[...]] * tiles, axis=1)
        p = jnp.exp2(s_ref[:, pl.ds(off, ATT_TK)] - m).astype(jnp.bfloat16)
        acc_ref[...] += _dot(p, v_ref[pl.ds(off, ATT_TK), :])
        return carry

    lax.fori_loop(0, SEQ // ATT_TK, weighted, 0, unroll=8)
    out = acc_ref[:, :B_HEAD_DIM] / acc_ref[:, B_HEAD_DIM:B_HEAD_DIM + 1]
    for g in range(group):
        o_ref[:, g * B_HEAD_DIM:(g + 1) * B_HEAD_DIM] = out[g * ATT_TQ:(g + 1) * ATT_TQ].astype(o_ref.dtype)


def _attn(qr, kr, vr):
    group = B_HEADS // B_KV_HEADS
    rows = group * ATT_TQ
    return pl.pallas_call(
        _attn_kernel,
        out_shape=jax.ShapeDtypeStruct((SEQ, B_WIDTH), jnp.bfloat16),
        grid=(B_KV_HEADS, SEQ // ATT_TQ),
        in_specs=[pl.BlockSpec((ATT_TQ, group * B_HEAD_DIM), lambda kv, i: (i, kv)),
                  pl.BlockSpec((SEQ, B_HEAD_DIM), lambda kv, i: (0, kv)),
                  pl.BlockSpec((SEQ, 2 * B_HEAD_DIM), lambda kv, i: (0, kv))],
        out_specs=pl.BlockSpec((ATT_TQ, group * B_HEAD_DIM), lambda kv, i: (i, kv)),
        scratch_shapes=[pltpu.VMEM((rows, SEQ), jnp.float32), pltpu.VMEM((rows, LANES), jnp.float32),
                        pltpu.VMEM((rows, 2 * B_HEAD_DIM), jnp.float32)],
        compiler_params=_cparams(("arbitrary", "arbitrary")),
        name="attn",
    )(qr, kr, vr)


def _mix_kernel(hf_ref, hb_ref, oa_ref, ng_ref, yb_ref, ga_ref, gb_ref, pa_ref, pb_ref, o_ref):
    parts = []
    for hd in range(A_HEADS):
        sl = slice(hd * A_HEAD_DIM, (hd + 1) * A_HEAD_DIM)
        hs = hf_ref[:, sl] + hb_ref[:, sl]
        y = hs * lax.rsqrt(jnp.mean(hs * hs, axis=-1, keepdims=True) + NORM_EPS) * ng_ref[:, sl]
        parts.append((_sigmoid(oa_ref[:, sl]) * y).astype(jnp.bfloat16))
    ya = jnp.concatenate(parts, axis=1)
    ta = _dot(ya, pa_ref[...])
    tb = _dot(yb_ref[...], pb_ref[...])
    o_ref[...] = (_sigmoid(ga_ref[...]) * ta + _sigmoid(gb_ref[...]) * tb).astype(o_ref.dtype)


def _mix(hf, hb, z, norm_g, yb, pa, pb):
    tm = MIX_TM
    return pl.pallas_call(
        _mix_kernel,
        out_shape=jax.ShapeDtypeStruct((SEQ, D_MODEL), jnp.bfloat16),
        grid=(SEQ // tm,),
        in_specs=[pl.BlockSpec((tm, A_WIDTH), lambda i: (i, 0)),
                  pl.BlockSpec((tm, A_WIDTH), lambda i: (i, 0)),
                  pl.BlockSpec((tm, A_WIDTH), lambda i: (i, Z_OA // A_WIDTH)),
                  pl.BlockSpec((1, A_WIDTH), lambda i: (0, 0)),
                  pl.BlockSpec((tm, B_WIDTH), lambda i: (i, 0)),
                  pl.BlockSpec((tm, D_MODEL), lambda i: (i, Z_GA // D_MODEL)),
                  pl.BlockSpec((tm, D_MODEL), lambda i: (i, Z_GB // D_MODEL)),
                  pl.BlockSpec((A_WIDTH, D_MODEL), lambda i: (0, 0)),
                  pl.BlockSpec((B_WIDTH, D_MODEL), lambda i: (0, 0))],
        out_specs=pl.BlockSpec((tm, D_MODEL), lambda i: (i, 0)),
        compiler_params=_cparams(("arbitrary",)),
        name="mix",
    )(hf, hb, z, norm_g, yb, z, z, pa, pb)


def _split_bf16(a):
    hi = a.astype(jnp.bfloat16)
    lo = (a - hi.astype(jnp.float32)).astype(jnp.bfloat16)
    return hi, lo


def _out_proj_kernel(mx_ref, w_ref, x_ref, g_ref, wr_ref, br_ref, h1_ref, xn_ref, lg_ref):
    h1 = x_ref[...] + _dot(mx_ref[...], w_ref[...])
    h1_ref[...] = h1
    xn = h1 * lax.rsqrt(jnp.mean(h1 * h1, axis=-1, keepdims=True) + NORM_EPS) * g_ref[...]
    xn_ref[...] = xn
    x_hi, x_lo = _split_bf16(xn)
    w_hi, w_lo = _split_bf16(wr_ref[...])
    lg_ref[...] = _dot_nt(w_hi, x_hi) + (_dot_nt(w_hi, x_lo) + _dot_nt(w_lo, x_hi)) + br_ref[...]


def _out_proj(mixed, w_out, x, g, w_router_t, b_router):
    tm = OUT_TM
    return pl.pallas_call(
        _out_proj_kernel,
        out_shape=(jax.ShapeDtypeStruct((SEQ, D_MODEL), jnp.float32),
                   jax.ShapeDtypeStruct((SEQ, D_MODEL), jnp.float32),
                   jax.ShapeDtypeStruct((N_EXPERTS, SEQ), jnp.float32)),
        grid=(SEQ // tm,),
        in_specs=[pl.BlockSpec((tm, D_MODEL), lambda i: (i, 0)),
                  pl.BlockSpec((D_MODEL, D_MODEL), lambda i: (0, 0)),
                  pl.BlockSpec((tm, D_MODEL), lambda i: (i, 0)),
                  pl.BlockSpec((1, D_MODEL), lambda i: (0, 0)),
                  pl.BlockSpec((N_EXPERTS, D_MODEL), lambda i: (0, 0)),
                  pl.BlockSpec((N_EXPERTS, 1), lambda i: (0, 0))],
        out_specs=(pl.BlockSpec((tm, D_MODEL), lambda i: (i, 0)),
                   pl.BlockSpec((tm, D_MODEL), lambda i: (i, 0)),
                   pl.BlockSpec((N_EXPERTS, tm), lambda i: (0, i))),
        compiler_params=_cparams(("arbitrary",)),
        name="out_proj",
    )(mixed, w_out, x, g, w_router_t, b_router)


def _route_kernel(lg_ref, e_ref, w_ref, pos_ref, cnt_ref, r_ref, carry_ref):
    tc = ROUTE_TC
    carry_ref[...] = jnp.zeros(carry_ref.shape, jnp.float32)
    eidx = lax.broadcasted_iota(jnp.int32, (N_EXPERTS, tc), 0)
    tri_r = lax.broadcasted_iota(jnp.int32, (tc, tc), 0)
    tri_c = lax.broadcasted_iota(jnp.int32, (tc, tc), 1)
    before = jnp.where(tri_r < tri_c, 1.0, 0.0).astype(jnp.bfloat16)

    def body(c, carry):
        off = pl.multiple_of(c * tc, tc)
        lg = lg_ref[:, pl.ds(off, tc)]
        vals, sels = [], []
        for _ in range(TOP_K):
            mx = jnp.max(lg, axis=0, keepdims=True)
            idx = jnp.min(jnp.where(lg == mx, eidx, N_EXPERTS), axis=0, keepdims=True)
            sel = eidx == idx
            vals.append(mx)
            sels.append(sel)
            lg = jnp.where(sel, -jnp.inf, lg)
        ex = [jnp.exp(v - vals[0]) for v in vals]
        tot = ex[0] + ex[1] + ex[2] + ex[3]
        chosen = jnp.where(sels[0] | sels[1] | sels[2] | sels[3], 1.0, 0.0)
        rank = _dot(chosen.astype(jnp.bfloat16), before) + carry_ref[:, 0:1]
        carry_ref[...] = carry_ref[...] + jnp.sum(chosen, axis=1, keepdims=True)
        for k in range(TOP_K):
            e_ref[k:k + 1, pl.ds(off, tc)] = jnp.sum(jnp.where(sels[k], eidx, 0), axis=0, keepdims=True)
            w_ref[k:k + 1, pl.ds(off, tc)] = ex[k] / tot
            r_ref[k:k + 1, pl.ds(off, tc)] = jnp.sum(jnp.where(sels[k], rank, 0.0), axis=0, keepdims=True)
        return carry

    lax.fori_loop(0, SEQ // tc, body, 0)

    counts = carry_ref[...].astype(jnp.int32)
    cnt_ref[...] = counts
    sub_shift = MOE_SUB.bit_length() - 1
    padded = (((counts + (MOE_SUB - 1)) >> sub_shift) << sub_shift).astype(jnp.float32)
    start = (_cumsum_rows(padded) - padded)[:, 0:1]

    def place(c, carry):
        off = pl.multiple_of(c * tc, tc)
        for k in range(TOP_K):
            sel = eidx == e_ref[k:k + 1, pl.ds(off, tc)]
            base = jnp.sum(jnp.where(sel, start, 0.0), axis=0, keepdims=True)
            pos_ref[k:k + 1, pl.ds(off, tc)] = (base + r_ref[k:k + 1, pl.ds(off, tc)]).astype(jnp.int32)
        return carry

    lax.fori_loop(0, SEQ // tc, place, 0)


def _route(logits_t):
    return pl.pallas_call(
        _route_kernel,
        out_shape=(jax.ShapeDtypeStruct((TOP_K, SEQ), jnp.int32),
                   jax.ShapeDtypeStruct((TOP_K, SEQ), jnp.float32),
                   jax.ShapeDtypeStruct((TOP_K, SEQ), jnp.int32),
                   jax.ShapeDtypeStruct((N_EXPERTS, LANES), jnp.int32)),
        scratch_shapes=[pltpu.VMEM((TOP_K, SEQ), jnp.float32), pltpu.VMEM((N_EXPERTS, LANES), jnp.float32)],
        compiler_params=_cparams(None),
        name="route",
    )(logits_t)


def _dispatch_kernel(pos_ref, xw_ref, xg_ref, sem):
    base = pl.program_id(0) * DISP_TT

    def copy(j, k):
        return pltpu.make_async_copy(xw_ref.at[pl.ds(j, 1)], xg_ref.at[pl.ds(pos_ref[k * SEQ + base + j], 1)], sem)

    def start(j, c):
        for k in range(TOP_K):
            copy(j, k).start()
        return c

    def wait(j, c):
        for k in range(TOP_K):
            copy(j, k).wait()
        return c

    lax.fori_loop(0, DISP_TT, start, 0, unroll=8)
    lax.fori_loop(0, DISP_TT, wait, 0, unroll=8)


def _dispatch(pos_flat, xw):
    return pl.pallas_call(
        _dispatch_kernel,
        out_shape=jax.ShapeDtypeStruct((MOE_ROWS, D_MODEL), jnp.float32),
        grid_spec=pltpu.PrefetchScalarGridSpec(
            num_scalar_prefetch=1,
            grid=(SEQ // DISP_TT,),
            in_specs=[pl.BlockSpec((DISP_TT, D_MODEL), lambda i, pos: (i, 0))],
            out_specs=pl.BlockSpec(memory_space=pl.ANY),
            scratch_shapes=[pltpu.SemaphoreType.DMA(())]),
        compiler_params=_cparams(("arbitrary",)),
        name="dispatch",
    )(pos_flat, xw)


def _experts_kernel(te_ref, sb_ref, ns_ref, nv_ref, na_ref, *refs):
    x_refs = refs[:MOE_NSUB]
    wg_ref, wu_ref, bg_ref, bu_ref, wd_ref, bd_ref, yg_ref, acc_ref, xs_ref, sem = refs[MOE_NSUB:]
    t = pl.program_id(0)
    f = pl.program_id(1)
    nf = pl.num_programs(1)

    def out_copy(first_group, j):
        rows = pl.ds(pl.multiple_of((first_group + j) * MOE_SUB, MOE_SUB), MOE_SUB)
        return pltpu.make_async_copy(acc_ref.at[pl.ds(j * MOE_SUB, MOE_SUB)], yg_ref.at[rows], sem)

    @pl.when(t < na_ref[0])
    def _():
        groups = ns_ref[t]
        valid = nv_ref[t]

        @pl.when(f == 0)
        def _():
            for j, x_ref in enumerate(x_refs):
                row = j * MOE_SUB + lax.broadcasted_iota(jnp.int32, x_ref.shape, 0)
                xs_ref[j * MOE_SUB:(j + 1) * MOE_SUB, :] = jnp.where(row < valid, x_ref[...], 0.0).astype(jnp.bfloat16)

        def ffn(n):
            rows = n * MOE_SUB
            x = xs_ref[:rows, :]
            gate = _dot(x, wg_ref[...].astype(jnp.bfloat16)) + bg_ref[...]
            up = _dot(x, wu_ref[...].astype(jnp.bfloat16)) + bu_ref[...]
            gate = jnp.minimum(gate, SWIGLU_LIMIT)
            up = jnp.clip(up, -SWIGLU_LIMIT, SWIGLU_LIMIT)
            act = ((up + 1.0) * (gate * _sigmoid(gate * SWIGLU_ALPHA))).astype(jnp.bfloat16)

            @pl.when(f == 0)
            def _():
                @pl.when(t > 0)
                def _():
                    for j in range(MOE_NSUB):
                        @pl.when(j < ns_ref[t - 1])
                        def _():
                            out_copy(sb_ref[t - 1], j).wait()

                acc_ref[:rows, :] = jnp.zeros((rows, D_MODEL), jnp.float32) + bd_ref[...]

            acc_ref[:rows, :] += _dot(act, wd_ref[...].astype(jnp.bfloat16))

            @pl.when(f == nf - 1)
            def _():
                for j in range(n):
                    out_copy(sb_ref[t], j).start()

                @pl.when(t == na_ref[0] - 1)
                def _():
                    for j in range(n):
                        out_copy(sb_ref[t], j).wait()

        for n in range(1, MOE_NSUB + 1):
            pl.when(groups == n)(functools.partial(ffn, n))


def _experts(tile_expert, tile_group0, tile_groups, tile_valid, n_active, xg, w_gate_up, b_gate_up, w_down, b_down):
    nf = D_FF // MOE_TF
    tf = MOE_TF
    last_group = MOE_ROWS // MOE_SUB - 1

    def tile(t, na):
        return jnp.minimum(t, na[0] - 1)

    def fcol(t, f, na):
        return jnp.where(t < na[0], f, nf - 1)

    def group(j):
        def index_map(t, f, te, sb, ns, nv, na):
            return (jnp.minimum(sb[tile(t, na)] + j, last_group), 0)
        return index_map

    def weight(row_block, col_block):
        def index_map(t, f, te, sb, ns, nv, na):
            fc = fcol(t, f, na)
            return (0, te[tile(t, na)], row_block(fc), col_block(fc))
        return index_map

    zero = lambda fc: 0
    same = lambda fc: fc
    upper = lambda fc: nf + fc
    return pl.pallas_call(
        _experts_kernel,
        out_shape=jax.ShapeDtypeStruct((MOE_ROWS, D_MODEL), jnp.float32),
        grid_spec=pltpu.PrefetchScalarGridSpec(
            num_scalar_prefetch=5,
            grid=(MOE_TILES, nf),
            in_specs=[pl.BlockSpec((MOE_SUB, D_MODEL), group(j)) for j in range(MOE_NSUB)] + [
                pl.BlockSpec((None, None, D_MODEL, tf), weight(zero, same)),
                pl.BlockSpec((None, None, D_MODEL, tf), weight(zero, upper)),
                pl.BlockSpec((None, None, 1, tf), weight(zero, same)),
                pl.BlockSpec((None, None, 1, tf), weight(zero, upper)),
                pl.BlockSpec((None, None, tf, D_MODEL), weight(same, zero)),
                pl.BlockSpec((None, None, 1, D_MODEL), weight(zero, zero)),
            ],
            out_specs=pl.BlockSpec(memory_space=pl.ANY),
            scratch_shapes=[pltpu.VMEM((MOE_TM, D_MODEL), jnp.float32), pltpu.VMEM((MOE_TM, D_MODEL), jnp.bfloat16),
                            pltpu.SemaphoreType.DMA(())]),
        compiler_params=_cparams(("arbitrary", "arbitrary")),
        name="experts",
    )(tile_expert, tile_group0, tile_groups, tile_valid, n_active, *([xg] * MOE_NSUB),
      w_gate_up, w_gate_up, b_gate_up, b_gate_up, w_down, b_down)


def _combine_kernel(pos_ref, yg_ref, w_ref, h1_ref, g_ref, o_ref, buf_ref, sem):
    tt = COMB_TT
    base = pl.program_id(0) * tt

    def copy(j, k):
        return pltpu.make_async_copy(yg_ref.at[pl.ds(pos_ref[k * SEQ + base + j], 1)],
                                     buf_ref.at[k, pl.ds(j, 1)], sem)

    def start(j, c):
        for k in range(TOP_K):
            copy(j, k).start()
        return c

    def wait(j, c):
        for k in range(TOP_K):
            copy(j, k).wait()
        return c

    lax.fori_loop(0, tt, start, 0, unroll=8)
    lax.fori_loop(0, tt, wait, 0, unroll=8)
    w = w_ref[...]
    h2 = h1_ref[...]
    for k in range(TOP_K):
        h2 = h2 + w[:, k:k + 1] * buf_ref[k]
    o_ref[...] = h2 * lax.rsqrt(jnp.mean(h2 * h2, axis=-1, keepdims=True) + NORM_EPS) * g_ref[...]


def _combine(pos_flat, yg, gate_w, h1, g):
    tt = COMB_TT
    return pl.pallas_call(
        _combine_kernel,
        out_shape=jax.ShapeDtypeStruct((SEQ, D_MODEL), jnp.float32),
        grid_spec=pltpu.PrefetchScalarGridSpec(
            num_scalar_prefetch=1,
            grid=(SEQ // tt,),
            in_specs=[pl.BlockSpec(memory_space=pl.ANY),
                      pl.BlockSpec((tt, TOP_K), lambda i, pos: (i, 0)),
                      pl.BlockSpec((tt, D_MODEL), lambda i, pos: (i, 0)),
                      pl.BlockSpec((1, D_MODEL), lambda i, pos: (0, 0))],
            out_specs=pl.BlockSpec((tt, D_MODEL), lambda i, pos: (i, 0)),
            scratch_shapes=[pltpu.VMEM((TOP_K, tt, D_MODEL), jnp.float32), pltpu.SemaphoreType.DMA(())]),
        compiler_params=_cparams(("arbitrary",)),
        name="combine",
    )(pos_flat, yg, gate_w, h1, g)


def _regroup_w_in(w):
    qa_oa = w[:, :4 * A_WIDTH]
    g0 = 4 * A_WIDTH
    gates = w[:, g0:g0 + 4 * A_HEADS]
    b0 = g0 + 4 * A_HEADS
    qb = w[:, b0:b0 + B_WIDTH]
    kv = w[:, b0 + B_WIDTH:b0 + B_WIDTH + 2 * B_KV_WIDTH]
    gm = w[:, b0 + B_WIDTH + 2 * B_KV_WIDTH:]
    main = jnp.concatenate([qa_oa, gm, qb, kv], axis=1).astype(jnp.bfloat16)
    gates = jnp.pad(gates, ((0, 0), (0, LANES - 4 * A_HEADS))).astype(jnp.bfloat16)
    return main, gates


def _tile_metadata(counts):
    as_i32 = lambda a: a.astype(jnp.int32)
    groups = (counts + MOE_SUB - 1) // MOE_SUB
    group_end = jnp.cumsum(groups)
    tiles = (groups + MOE_NSUB - 1) // MOE_NSUB
    tile_end = jnp.cumsum(tiles)
    t = jnp.arange(MOE_TILES, dtype=jnp.int32)
    expert = jnp.minimum(jnp.sum(t[:, None] >= tile_end[None, :], axis=1), N_EXPERTS - 1)
    local = t - (tile_end - tiles)[expert]
    group0 = (group_end - groups)[expert] + MOE_NSUB * local
    n_groups = jnp.clip(groups[expert] - MOE_NSUB * local, 0, MOE_NSUB)
    valid = jnp.clip(counts[expert] - MOE_TM * local, 0, MOE_TM)
    return as_i32(expert), as_i32(group0), as_i32(n_groups), as_i32(valid), as_i32(tile_end[-1:])


def kernel(x, norm_mix_g, w_in, conv_w, conv_b, b_mlstm_gates, mlstm_norm_g, q_norm_g, k_norm_g, w_proj_a,
           w_proj_b, w_out, norm_ffn_g, w_router, b_router, w_gate_up, b_gate_up, w_down, b_down, norm_final_g):
    assert x.shape == (1, SEQ, D_MODEL) and w_in.shape[0] == 1
    x2 = x[0]
    w_main, w_gates = _regroup_w_in(w_in[0])
    z, gates = _in_proj(x2, norm_mix_g, w_main, w_gates)

    (qc,) = _conv(z, conv_w[0], conv_b, 0, A_HEAD_DIM ** -0.5, False)
    kc, kct = _conv(z, conv_w[0], conv_b, A_WIDTH // CONV_C, 1.0, True)
    gate_bias = jnp.pad(b_mlstm_gates.reshape(1, 4 * A_HEADS), ((0, 0), (0, LANES - 4 * A_HEADS)))
    hf, hb = _mlstm(qc, kc, kct, z, gates, gate_bias)

    half = B_HEAD_DIM // 2
    inv_freq = ROPE_THETA ** (-jnp.arange(0, half, 2, dtype=jnp.float32) / half)
    qr, kr, vr = _attn_prep(z, q_norm_g, k_norm_g, jnp.tile(inv_freq, 4).reshape(1, LANES))
    yb = _attn(qr, kr, vr)

    mixed = _mix(hf, hb, z, mlstm_norm_g, yb, w_proj_a[0].astype(jnp.bfloat16), w_proj_b[0].astype(jnp.bfloat16))
    h1, xw, logits_t = _out_proj(mixed, w_out[0].astype(jnp.bfloat16), x2, norm_ffn_g, w_router[0].T,
                                 b_router.reshape(N_EXPERTS, 1))

    _, gate_w, pos, counts = _route(logits_t)
    pos = pos.reshape(-1)
    tile_meta = _tile_metadata(counts[:, 0])

    xg = _dispatch(pos, xw)
    yg = _experts(*tile_meta, xg, w_gate_up, b_gate_up.reshape(1, N_EXPERTS, 1, 2 * D_FF),
                  w_down, b_down.reshape(1, N_EXPERTS, 1, D_MODEL))
    out = _combine(pos, yg, gate_w.T, h1, norm_final_g.reshape(1, D_MODEL))
    return out[None]
```

```python
import functools

import jax
import jax.numpy as jnp
import numpy as np
from jax import lax
from jax.experimental import pallas as pl
from jax.experimental.pallas import tpu as pltpu

D_MODEL = 2048
SEQ = 8192
GRID_W = 64
NORM_EPS = 1e-6
A_HEADS = 4
A_HEAD_DIM = 256
A_WIDTH = A_HEADS * A_HEAD_DIM
CONV_W = 5
B_HEADS = 8
B_KV_HEADS = 2
B_HEAD_DIM = 128
B_WIDTH = B_HEADS * B_HEAD_DIM
B_KV_WIDTH = B_KV_HEADS * B_HEAD_DIM
ROPE_THETA = 10000.0
N_EXPERTS = 32
TOP_K = 4
D_FF = D_MODEL
SWIGLU_LIMIT = 7.0
SWIGLU_ALPHA = 1.702
LOG2_E = 1.4426950408889634

LANES = 128
SUBLANES = 8
VMEM_LIMIT = 56 * 1024 * 1024
EXPERTS_VMEM_LIMIT = 60 * 1024 * 1024

Z_QK = 0
Z_VA = 2048
Z_OA = 3072
Z_GA = 4096
Z_GB = 6144
Z_QB = 8192
Z_KB = 9216
Z_VB = 9472
Z_WIDTH = 9728

PROJ_TM, PROJ_TN = 1024, 512
CONV_R, CONV_C = 1024, 256
A_CHUNK = 256
PREP_TM = 512
ATT_TQ, ATT_TK = 128, 512
MIX_TM = 256
OUT_TM = 256
ROUTE_TC = 512
MOE_SUB = 256
MOE_NSUB = 4
MOE_TM = MOE_SUB * MOE_NSUB
MOE_TF = 512
MOE_ROWS = SEQ * TOP_K + N_EXPERTS * MOE_SUB
MOE_TILES = SEQ * TOP_K // MOE_TM + N_EXPERTS
DISP_TT = 512
COMB_TT = 256


def _cparams(sem, vmem=VMEM_LIMIT):
    return pltpu.CompilerParams(dimension_semantics=sem, vmem_limit_bytes=vmem)


def _sigmoid(x):
    return 1.0 / (1.0 + jnp.exp(-x))


def _dot(a, b):
    return jnp.dot(a, b, preferred_element_type=jnp.float32)


def _dot_nt(a, b):
    return lax.dot_general(a, b, (((1,), (1,)), ((), ())), preferred_element_type=jnp.float32)


def _in_proj_kernel(x_ref, g_ref, w_ref, wg_ref, z_ref, gate_ref, u_ref):
    @pl.when(pl.program_id(1) == 0)
    def _():
        x = x_ref[...]
        y = x * lax.rsqrt(jnp.mean(x * x, axis=-1, keepdims=True) + NORM_EPS) * g_ref[...]
        u = y.astype(jnp.bfloat16)
        u_ref[...] = u
        gate_ref[...] = _dot(u, wg_ref[...])

    z_ref[...] = _dot(u_ref[...], w_ref[...])


def _in_proj(x, g, w_main, w_gate):
    grid = (SEQ // PROJ_TM, Z_WIDTH // PROJ_TN)
    return pl.pallas_call(
        _in_proj_kernel,
        out_shape=(jax.ShapeDtypeStruct((SEQ, Z_WIDTH), jnp.float32),
                   jax.ShapeDtypeStruct((SEQ, LANES), jnp.float32)),
        grid=grid,
        in_specs=[pl.BlockSpec((PROJ_TM, D_MODEL), lambda i, j: (i, 0)),
                  pl.BlockSpec((1, D_MODEL), lambda i, j: (0, 0)),
                  pl.BlockSpec((D_MODEL, PROJ_TN), lambda i, j: (0, j)),
                  pl.BlockSpec((D_MODEL, LANES), lambda i, j: (0, 0))],
        out_specs=(pl.BlockSpec((PROJ_TM, PROJ_TN), lambda i, j: (i, j)),
                   pl.BlockSpec((PROJ_TM, LANES), lambda i, j: (i, 0))),
        scratch_shapes=[pltpu.VMEM((PROJ_TM, D_MODEL), jnp.bfloat16)],
        compiler_params=_cparams(("arbitrary", "arbitrary")),
        name="in_proj",
    )(x, g, w_main, w_gate)


def _conv_kernel(prev_ref, main_ref, next_ref, w_ref, b_ref, *out_refs, scale, transposed):
    i = pl.program_id(0)
    n = pl.num_programs(0)
    prev = jnp.where(i > 0, prev_ref[...], 0.0)
    nxt = jnp.where(i < n - 1, next_ref[...], 0.0)
    ext = jnp.concatenate([prev, main_ref[...], nxt], axis=0)
    rows = CONV_R + 2 * SUBLANES
    w = w_ref[...]
    acc = jnp.zeros((CONV_R, CONV_C), jnp.float32) + b_ref[...]
    for j in range(CONV_W):
        shift = (CONV_W // 2 - j) % rows
        xs = ext if shift == 0 else pltpu.roll(ext, shift, axis=0)
        acc = acc + xs[SUBLANES:SUBLANES + CONV_R, :] * w[j:j + 1, :]
    y = acc * _sigmoid(acc) * scale
    out_refs[0][...] = y.astype(jnp.bfloat16)
    if transposed:
        out_refs[1][...] = y.T.astype(jnp.bfloat16)


def _conv(z, conv_w, conv_b, col_block0, scale, transposed):
    nr, ncb = SEQ // CONV_R, A_WIDTH // CONV_C
    rb = CONV_R // SUBLANES
    last = SEQ // SUBLANES - 1
    out_shape = [jax.ShapeDtypeStruct((SEQ, A_WIDTH), jnp.bfloat16)]
    out_specs = [pl.BlockSpec((CONV_R, CONV_C), lambda i, c: (i, c))]
    if transposed:
        out_shape.append(jax.ShapeDtypeStruct((A_WIDTH, SEQ), jnp.bfloat16))
        out_specs.append(pl.BlockSpec((CONV_C, CONV_R), lambda i, c: (c, i)))
    return pl.pallas_call(
        functools.partial(_conv_kernel, scale=scale, transposed=transposed),
        out_shape=tuple(out_shape),
        grid=(nr, ncb),
        in_specs=[pl.BlockSpec((SUBLANES, CONV_C), lambda i, c: (jnp.maximum(i * rb - 1, 0), col_block0 + c)),
                  pl.BlockSpec((CONV_R, CONV_C), lambda i, c: (i, col_block0 + c)),
                  pl.BlockSpec((SUBLANES, CONV_C), lambda i, c: (jnp.minimum((i + 1) * rb, last), col_block0 + c)),
                  pl.BlockSpec((CONV_W, CONV_C), lambda i, c: (0, col_block0 + c)),
                  pl.BlockSpec((1, CONV_C), lambda i, c: (0, col_block0 + c))],
        out_specs=tuple(out_specs),
        compiler_params=_cparams(("arbitrary", "arbitrary")),
        name="conv_k" if transposed else "conv_q",
    )(z, z, z, conv_w, conv_b)


def _log_sigmoid(x):
    return jnp.minimum(x, 0.0) - jnp.log(1.0 + jnp.exp(-jnp.abs(x)))


def _cumsum_rows(x):
    n = x.shape[0]
    row = lax.broadcasted_iota(jnp.int32, x.shape, 0)
    s = 1
    while s < n:
        x = x + jnp.where(row >= s, pltpu.roll(x, s, axis=0), 0.0)
        s *= 2
    return x


def _mlstm_gate_tables(g_ref, bias_ref, backward):
    L = A_CHUNK
    pre = g_ref[...] + bias_ref[...]
    logf = _log_sigmoid(pre)
    pref = _cumsum_rows(logf)
    total = pref[L - 1:L, :]
    cum = (total - pref + logf) if backward else pref
    lane = lax.broadcasted_iota(jnp.int32, (L, LANES), 1)
    f0 = (3 if backward else 1) * A_HEADS
    tab = jnp.where((lane >= f0) & (lane < f0 + A_HEADS), cum, pre)
    return tab, tab.T, total


def _mlstm_chain(q_ref, k_ref, kt_ref, v_ref, tables, h_ref, c_ref, n_ref, m_ref, h, backward):
    L, dh = A_CHUNK, A_HEAD_DIM
    tab, tab_t, total = tables
    ci = (2 if backward else 0) * A_HEADS + h
    cf = (3 if backward else 1) * A_HEADS + h
    i_c, b_c = tab[:, ci:ci + 1], tab[:, cf:cf + 1]
    i_r, b_r = tab_t[ci:ci + 1, :], tab_t[cf:cf + 1, :]
    b_last = total[:, cf:cf + 1]
    hs = slice(h * dh, (h + 1) * dh)
    q_ref, k_ref, v_ref, h_ref = q_ref.at[:, hs], k_ref.at[:, hs], v_ref.at[:, hs], h_ref.at[:, hs]
    kt_ref, c_ref, n_ref, m_ref = kt_ref.at[hs, :], c_ref.at[h], n_ref.at[h], m_ref.at[h]
    q = q_ref[...]
    k = k_ref[...]
    v = v_ref[...].astype(jnp.bfloat16)
    t_idx = lax.broadcasted_iota(jnp.int32, (L, L), 0)
    s_idx = lax.broadcasted_iota(jnp.int32, (L, L), 1)
    visible = (s_idx >= t_idx) if backward else (s_idx <= t_idx)
    d = jnp.where(visible, b_c - b_r + i_r, -jnp.inf)
    m_prev = m_ref[0:1, 0:1]
    m_inter = b_c + m_prev
    m_t = jnp.maximum(m_inter, jnp.max(d, axis=1, keepdims=True))
    w = jnp.exp(d - m_t) * _dot_nt(q, k)
    s_inter = jnp.exp(m_inter - m_t)
    c_old = c_ref[...]
    n_old = n_ref[0:1, :]
    num = _dot(w.astype(jnp.bfloat16), v) + s_inter * _dot(q, c_old.astype(jnp.bfloat16))
    qn = jnp.sum(q.astype(jnp.float32) * n_old, axis=1, keepdims=True)
    den = jnp.sum(w, axis=1, keepdims=True) + s_inter * qn
    h_ref[...] = num / jnp.maximum(jnp.abs(den), jnp.exp(-m_t))

    g_c = b_last - b_c + i_c
    g_r = b_last - b_r + i_r
    m_new = jnp.maximum(b_last + m_prev, jnp.max(g_r, axis=1, keepdims=True))
    wk_c = jnp.exp(g_c - m_new)
    wk_r = jnp.exp(g_r - m_new)
    decay = jnp.exp(b_last + m_prev - m_new)
    ktw = (kt_ref[...].astype(jnp.float32) * wk_r).astype(jnp.bfloat16)
    c_ref[...] = decay * c_old + _dot(ktw, v)
    n_new = decay * n_old + jnp.sum(k.astype(jnp.float32) * wk_c, axis=0, keepdims=True)
    n_ref[...] = jnp.broadcast_to(n_new, n_ref.shape)
    m_ref[...] = jnp.broadcast_to(m_new, m_ref.shape)


def _mlstm_kernel(qf, kf, ktf, vf, gf, qb, kb, ktb, vb, gb, bias_ref, hf_ref, hb_ref,
                  cf_ref, nf_ref, mf_ref, cb_ref, nb_ref, mb_ref):
    @pl.when(pl.program_id(0) == 0)
    def _():
        for r in (cf_ref, nf_ref, mf_ref, cb_ref, nb_ref, mb_ref):
            r[...] = jnp.zeros(r.shape, r.dtype)

    tf = _mlstm_gate_tables(gf, bias_ref, False)
    tb = _mlstm_gate_tables(gb, bias_ref, True)
    for h in range(A_HEADS):
        _mlstm_chain(qf, kf, ktf, vf, tf, hf_ref, cf_ref, nf_ref, mf_ref, h, False)
        _mlstm_chain(qb, kb, ktb, vb, tb, hb_ref, cb_ref, nb_ref, mb_ref, h, True)


def _mlstm(qc, kc, kct, z, gates, bias):
    L, dh = A_CHUNK, A_HEAD_DIM
    nc = SEQ // L

    def specs(chunk):
        return [pl.BlockSpec((L, A_WIDTH), lambda j: (chunk(j), 0)),
                pl.BlockSpec((L, A_WIDTH), lambda j: (chunk(j), 0)),
                pl.BlockSpec((A_WIDTH, L), lambda j: (0, chunk(j))),
                pl.BlockSpec((L, A_WIDTH), lambda j: (chunk(j), Z_VA // A_WIDTH)),
                pl.BlockSpec((L, LANES), lambda j: (chunk(j), 0))]

    fwd = lambda j: j
    bwd = lambda j: nc - 1 - j
    state = [pltpu.VMEM((A_HEADS, dh, dh), jnp.float32), pltpu.VMEM((A_HEADS, SUBLANES, dh), jnp.float32),
             pltpu.VMEM((A_HEADS, SUBLANES, LANES), jnp.float32)]
    return pl.pallas_call(
        _mlstm_kernel,
        out_shape=(jax.ShapeDtypeStruct((SEQ, A_WIDTH), jnp.float32),
                   jax.ShapeDtypeStruct((SEQ, A_WIDTH), jnp.float32)),
        grid=(nc,),
        in_specs=specs(fwd) + specs(bwd) + [pl.BlockSpec((1, LANES), lambda j: (0, 0))],
        out_specs=(pl.BlockSpec((L, A_WIDTH), lambda j: (fwd(j), 0)),
                   pl.BlockSpec((L, A_WIDTH), lambda j: (bwd(j), 0))),
        scratch_shapes=state + state,
        compiler_params=_cparams(("arbitrary",)),
        name="mlstm",
    )(qc, kc, kct, z, gates, qc, kc, kct, z, gates, bias)


def _attn_prep_kernel(q_ref, k_ref, v_ref, qg_ref, kg_ref, invf_ref, qo_ref, ko_ref, vo_ref):
    tm = q_ref.shape[0]
    t = pl.program_id(0) * tm + lax.broadcasted_iota(jnp.int32, (tm, LANES), 0)
    lane = lax.broadcasted_iota(jnp.int32, (tm, LANES), 1)
    half = B_HEAD_DIM // 2
    grid_shift = GRID_W.bit_length() - 1
    pos = jnp.where(lane < half, t >> grid_shift, t & (GRID_W - 1)).astype(jnp.float32)
    ang = pos * invf_ref[...]
    cos = jnp.cos(ang)
    first = (lane & (half - 1)) < (half // 2)
    sin = jnp.where(first, -jnp.sin(ang), jnp.sin(ang))

    def norm_rope(x, g, scale):
        y = x * lax.rsqrt(jnp.mean(x * x, axis=-1, keepdims=True) + NORM_EPS) * g
        partner = jnp.where(first, pltpu.roll(y, LANES - half // 2, axis=1), pltpu.roll(y, half // 2, axis=1))
        return ((y * cos + partner * sin) * scale).astype(jnp.bfloat16)

    for hd in range(B_HEADS):
        sl = slice(hd * B_HEAD_DIM, (hd + 1) * B_HEAD_DIM)
        qo_ref[:, sl] = norm_rope(q_ref[:, sl], qg_ref[...], B_HEAD_DIM ** -0.5 * LOG2_E)
    ones_col = jnp.where(lane == 0, 1.0, 0.0).astype(jnp.bfloat16)
    for hd in range(B_KV_HEADS):
        sl = slice(hd * B_HEAD_DIM, (hd + 1) * B_HEAD_DIM)
        ko_ref[:, sl] = norm_rope(k_ref[:, sl], kg_ref[...], 1.0)
        vo_ref[:, 2 * hd * B_HEAD_DIM:(2 * hd + 1) * B_HEAD_DIM] = v_ref[:, sl].astype(jnp.bfloat16)
        vo_ref[:, (2 * hd + 1) * B_HEAD_DIM:(2 * hd + 2) * B_HEAD_DIM] = ones_col


def _attn_prep(z, q_norm_g, k_norm_g, inv_freq):
    tm = PREP_TM
    return pl.pallas_call(
        _attn_prep_kernel,
        out_shape=(jax.ShapeDtypeStruct((SEQ, B_WIDTH), jnp.bfloat16),
                   jax.ShapeDtypeStruct((SEQ, B_KV_WIDTH), jnp.bfloat16),
                   jax.ShapeDtypeStruct((SEQ, 2 * B_KV_WIDTH), jnp.bfloat16)),
        grid=(SEQ // tm,),
        in_specs=[pl.BlockSpec((tm, B_WIDTH), lambda i: (i, Z_QB // B_WIDTH)),
                  pl.BlockSpec((tm, B_KV_WIDTH), lambda i: (i, Z_KB // B_KV_WIDTH)),
                  pl.BlockSpec((tm, B_KV_WIDTH), lambda i: (i, Z_VB // B_KV_WIDTH)),
                  pl.BlockSpec((1, LANES), lambda i: (0, 0)),
                  pl.BlockSpec((1, LANES), lambda i: (0, 0)),
                  pl.BlockSpec((1, LANES), lambda i: (0, 0))],
        out_specs=(pl.BlockSpec((tm, B_WIDTH), lambda i: (i, 0)),
                   pl.BlockSpec((tm, B_KV_WIDTH), lambda i: (i, 0)),
                   pl.BlockSpec((tm, 2 * B_KV_WIDTH), lambda i: (i, 0))),
        compiler_params=_cparams(("arbitrary",)),
        name="attn_prep",
    )(z, z, z, q_norm_g, k_norm_g, inv_freq)


def _attn_kernel(q_ref, k_ref, v_ref, o_ref, s_ref, m_ref, acc_ref):
    group = B_HEADS // B_KV_HEADS
    tiles = ATT_TK // LANES
    q = jnp.concatenate([q_ref[:, g * B_HEAD_DIM:(g + 1) * B_HEAD_DIM] for g in range(group)], axis=0)
    m_ref[...] = jnp.full(m_ref.shape, -jnp.inf, jnp.float32)
    acc_ref[...] = jnp.zeros(acc_ref.shape, jnp.float32)

    def scores(c, carry):
        off = pl.multiple_of(c * ATT_TK, ATT_TK)
        s = _dot_nt(q, k_ref[pl.ds(off, ATT_TK), :])
        s_ref[:, pl.ds(off, ATT_TK)] = s
        part = s[:, :LANES]
        for j in range(1, tiles):
            part = jnp.maximum(part, s[:, j * LANES:(j + 1) * LANES])
        m_ref[...] = jnp.maximum(m_ref[...], part)
        return carry

    lax.fori_loop(0, SEQ // ATT_TK, scores, 0, unroll=8)
    row_max = jnp.max(m_ref[...], axis=1, keepdims=True)
    m_ref[...] = jnp.broadcast_to(row_max, m_ref.shape)

    def weighted(c, carry):
        off = pl.multiple_of(c * ATT_TK, ATT_TK)
        m = jnp.concatenate([m_ref[...]] * tiles, axis=1)
        p = jnp.exp2(s_ref[:, pl.ds(off, ATT_TK)] - m).astype(jnp.bfloat16)
        acc_ref[...] += _dot(p, v_ref[pl.ds(off, ATT_TK), :])
        return carry

    lax.fori_loop(0, SEQ // ATT_TK, weighted, 0, unroll=8)
    out = acc_ref[:, :B_HEAD_DIM] / acc_ref[:, B_HEAD_DIM:B_HEAD_DIM + 1]
    for g in range(group):
        o_ref[:, g * B_HEAD_DIM:(g + 1) * B_HEAD_DIM] = out[g * ATT_TQ:(g + 1) * ATT_TQ].astype(o_ref.dtype)


def _attn(qr, kr, vr):
    group = B_HEADS // B_KV_HEADS
    rows = group * ATT_TQ
    return pl.pallas_call(
        _attn_kernel,
        out_shape=jax.ShapeDtypeStruct((SEQ, B_WIDTH), jnp.bfloat16),
        grid=(B_KV_HEADS, SEQ // ATT_TQ),
        in_specs=[pl.BlockSpec((ATT_TQ, group * B_HEAD_DIM), lambda kv, i: (i, kv)),
                  pl.BlockSpec((SEQ, B_HEAD_DIM), lambda kv, i: (0, kv)),
                  pl.BlockSpec((SEQ, 2 * B_HEAD_DIM), lambda kv, i: (0, kv))],
        out_specs=pl.BlockSpec((ATT_TQ, group * B_HEAD_DIM), lambda kv, i: (i, kv)),
        scratch_shapes=[pltpu.VMEM((rows, SEQ), jnp.float32), pltpu.VMEM((rows, LANES), jnp.float32),
                        pltpu.VMEM((rows, 2 * B_HEAD_DIM), jnp.float32)],
        compiler_params=_cparams(("arbitrary", "arbitrary")),
        name="attn",
    )(qr, kr, vr)


def _mix_kernel(hf_ref, hb_ref, oa_ref, ng_ref, yb_ref, ga_ref, gb_ref, pa_ref, pb_ref, o_ref):
    parts = []
    for hd in range(A_HEADS):
        sl = slice(hd * A_HEAD_DIM, (hd + 1) * A_HEAD_DIM)
        hs = hf_ref[:, sl] + hb_ref[:, sl]
        y = hs * lax.rsqrt(jnp.mean(hs * hs, axis=-1, keepdims=True) + NORM_EPS) * ng_ref[:, sl]
        parts.append((_sigmoid(oa_ref[:, sl]) * y).astype(jnp.bfloat16))
    ya = jnp.concatenate(parts, axis=1)
    ta = _dot(ya, pa_ref[...])
    tb = _dot(yb_ref[...], pb_ref[...])
    o_ref[...] = (_sigmoid(ga_ref[...]) * ta + _sigmoid(gb_ref[...]) * tb).astype(o_ref.dtype)


def _mix(hf, hb, z, norm_g, yb, pa, pb):
    tm = MIX_TM
    return pl.pallas_call(
        _mix_kernel,
        out_shape=jax.ShapeDtypeStruct((SEQ, D_MODEL), jnp.bfloat16),
        grid=(SEQ // tm,),
        in_specs=[pl.BlockSpec((tm, A_WIDTH), lambda i: (i, 0)),
                  pl.BlockSpec((tm, A_WIDTH), lambda i: (i, 0)),
                  pl.BlockSpec((tm, A_WIDTH), lambda i: (i, Z_OA // A_WIDTH)),
                  pl.BlockSpec((1, A_WIDTH), lambda i: (0, 0)),
                  pl.BlockSpec((tm, B_WIDTH), lambda i: (i, 0)),
                  pl.BlockSpec((tm, D_MODEL), lambda i: (i, Z_GA // D_MODEL)),
                  pl.BlockSpec((tm, D_MODEL), lambda i: (i, Z_GB // D_MODEL)),
                  pl.BlockSpec((A_WIDTH, D_MODEL), lambda i: (0, 0)),
                  pl.BlockSpec((B_WIDTH, D_MODEL), lambda i: (0, 0))],
        out_specs=pl.BlockSpec((tm, D_MODEL), lambda i: (i, 0)),
        compiler_params=_cparams(("arbitrary",)),
        name="mix",
    )(hf, hb, z, norm_g, yb, z, z, pa, pb)


def _split_bf16(a):
    hi = a.astype(jnp.bfloat16)
    lo = (a - hi.astype(jnp.float32)).astype(jnp.bfloat16)
    return hi, lo


def _out_proj_kernel(mx_ref, w_ref, x_ref, g_ref, wr_ref, br_ref, h1_ref, xn_ref, lg_ref):
    h1 = x_ref[...] + _dot(mx_ref[...], w_ref[...])
    h1_ref[...] = h1
    xn = h1 * lax.rsqrt(jnp.mean(h1 * h1, axis=-1, keepdims=True) + NORM_EPS) * g_ref[...]
    xn_ref[...] = xn
    x_hi, x_lo = _split_bf16(xn)
    w_hi, w_lo = _split_bf16(wr_ref[...])
    lg_ref[...] = _dot_nt(w_hi, x_hi) + (_dot_nt(w_hi, x_lo) + _dot_nt(w_lo, x_hi)) + br_ref[...]


def _out_proj(mixed, w_out, x, g, w_router_t, b_router):
    tm = OUT_TM
    return pl.pallas_call(
        _out_proj_kernel,
        out_shape=(jax.ShapeDtypeStruct((SEQ, D_MODEL), jnp.float32),
                   jax.ShapeDtypeStruct((SEQ, D_MODEL), jnp.float32),
                   jax.ShapeDtypeStruct((N_EXPERTS, SEQ), jnp.float32)),
        grid=(SEQ // tm,),
        in_specs=[pl.BlockSpec((tm, D_MODEL), lambda i: (i, 0)),
                  pl.BlockSpec((D_MODEL, D_MODEL), lambda i: (0, 0)),
                  pl.BlockSpec((tm, D_MODEL), lambda i: (i, 0)),
                  pl.BlockSpec((1, D_MODEL), lambda i: (0, 0)),
                  pl.BlockSpec((N_EXPERTS, D_MODEL), lambda i: (0, 0)),
                  pl.BlockSpec((N_EXPERTS, 1), lambda i: (0, 0))],
        out_specs=(pl.BlockSpec((tm, D_MODEL), lambda i: (i, 0)),
                   pl.BlockSpec((tm, D_MODEL), lambda i: (i, 0)),
                   pl.BlockSpec((N_EXPERTS, tm), lambda i: (0, i))),
        compiler_params=_cparams(("arbitrary",)),
        name="out_proj",
    )(mixed, w_out, x, g, w_router_t, b_router)


def _route_kernel(lg_ref, e_ref, w_ref, pos_ref, cnt_ref, r_ref, carry_ref):
    tc = ROUTE_TC
    carry_ref[...] = jnp.zeros(carry_ref.shape, jnp.float32)
    eidx = lax.broadcasted_iota(jnp.int32, (N_EXPERTS, tc), 0)
    tri_r = lax.broadcasted_iota(jnp.int32, (tc, tc), 0)
    tri_c = lax.broadcasted_iota(jnp.int32, (tc, tc), 1)
    before = jnp.where(tri_r < tri_c, 1.0, 0.0).astype(jnp.bfloat16)

    def body(c, carry):
        off = pl.multiple_of(c * tc, tc)
        lg = lg_ref[:, pl.ds(off, tc)]
        vals, sels = [], []
        for _ in range(TOP_K):
            mx = jnp.max(lg, axis=0, keepdims=True)
            idx = jnp.min(jnp.where(lg == mx, eidx, N_EXPERTS), axis=0, keepdims=True)
            sel = eidx == idx
            vals.append(mx)
            sels.append(sel)
            lg = jnp.where(sel, -jnp.inf, lg)
        ex = [jnp.exp(v - vals[0]) for v in vals]
        tot = ex[0] + ex[1] + ex[2] + ex[3]
        chosen = jnp.where(sels[0] | sels[1] | sels[2] | sels[3], 1.0, 0.0)
        rank = _dot(chosen.astype(jnp.bfloat16), before) + carry_ref[:, 0:1]
        carry_ref[...] = carry_ref[...] + jnp.sum(chosen, axis=1, keepdims=True)
        for k in range(TOP_K):
            e_ref[k:k + 1, pl.ds(off, tc)] = jnp.sum(jnp.where(sels[k], eidx, 0), axis=0, keepdims=True)
            w_ref[k:k + 1, pl.ds(off, tc)] = ex[k] / tot
            r_ref[k:k + 1, pl.ds(off, tc)] = jnp.sum(jnp.where(sels[k], rank, 0.0), axis=0, keepdims=True)
        return carry

    lax.fori_loop(0, SEQ // tc, body, 0)

    counts = carry_ref[...].astype(jnp.int32)
    cnt_ref[...] = counts
    sub_shift = MOE_SUB.bit_length() - 1
    padded = (((counts + (MOE_SUB - 1)) >> sub_shift) << sub_shift).astype(jnp.float32)
    start = (_cumsum_rows(padded) - padded)[:, 0:1]

    def place(c, carry):
        off = pl.multiple_of(c * tc, tc)
        for k in range(TOP_K):
            sel = eidx == e_ref[k:k + 1, pl.ds(off, tc)]
            base = jnp.sum(jnp.where(sel, start, 0.0), axis=0, keepdims=True)
            pos_ref[k:k + 1, pl.ds(off, tc)] = (base + r_ref[k:k + 1, pl.ds(off, tc)]).astype(jnp.int32)
        return carry

    lax.fori_loop(0, SEQ // tc, place, 0)


def _route(logits_t):
    return pl.pallas_call(
        _route_kernel,
        out_shape=(jax.ShapeDtypeStruct((TOP_K, SEQ), jnp.int32),
                   jax.ShapeDtypeStruct((TOP_K, SEQ), jnp.float32),
                   jax.ShapeDtypeStruct((TOP_K, SEQ), jnp.int32),
                   jax.ShapeDtypeStruct((N_EXPERTS, LANES), jnp.int32)),
        scratch_shapes=[pltpu.VMEM((TOP_K, SEQ), jnp.float32), pltpu.VMEM((N_EXPERTS, LANES), jnp.float32)],
        compiler_params=_cparams(None),
        name="route",
    )(logits_t)


def _dispatch_kernel(pos_ref, xw_ref, xg_ref, sem):
    base = pl.program_id(0) * DISP_TT

    def copy(j, k):
        return pltpu.make_async_copy(xw_ref.at[pl.ds(j, 1)], xg_ref.at[pl.ds(pos_ref[k * SEQ + base + j], 1)], sem)

    def start(j, c):
        for k in range(TOP_K):
            copy(j, k).start()
        return c

    def wait(j, c):
        for k in range(TOP_K):
            copy(j, k).wait()
        return c

    lax.fori_loop(0, DISP_TT, start, 0, unroll=8)
    lax.fori_loop(0, DISP_TT, wait, 0, unroll=8)


def _dispatch(pos_flat, xw):
    return pl.pallas_call(
        _dispatch_kernel,
        out_shape=jax.ShapeDtypeStruct((MOE_ROWS, D_MODEL), jnp.float32),
        grid_spec=pltpu.PrefetchScalarGridSpec(
            num_scalar_prefetch=1,
            grid=(SEQ // DISP_TT,),
            in_specs=[pl.BlockSpec((DISP_TT, D_MODEL), lambda i, pos: (i, 0))],
            out_specs=pl.BlockSpec(memory_space=pl.ANY),
            scratch_shapes=[pltpu.SemaphoreType.DMA(())]),
        compiler_params=_cparams(("arbitrary",)),
        name="dispatch",
    )(pos_flat, xw)


def _experts_kernel(te_ref, sb_ref, ns_ref, nv_ref, na_ref, *refs):
    x_refs = refs[:MOE_NSUB]
    wg_ref, wu_ref, bg_ref, bu_ref, wd_ref, bd_ref, yg_ref, acc_ref, xs_ref, sem = refs[MOE_NSUB:]
    t = pl.program_id(0)
    f = pl.program_id(1)
    nf = pl.num_programs(1)

    def out_copy(first_group, j):
        rows = pl.ds(pl.multiple_of((first_group + j) * MOE_SUB, MOE_SUB), MOE_SUB)
        return pltpu.make_async_copy(acc_ref.at[pl.ds(j * MOE_SUB, MOE_SUB)], yg_ref.at[rows], sem)

    @pl.when(t < na_ref[0])
    def _():
        groups = ns_ref[t]
        valid = nv_ref[t]

        @pl.when(f == 0)
        def _():
            for j, x_ref in enumerate(x_refs):
                row = j * MOE_SUB + lax.broadcasted_iota(jnp.int32, x_ref.shape, 0)
                xs_ref[j * MOE_SUB:(j + 1) * MOE_SUB, :] = jnp.where(row < valid, x_ref[...], 0.0).astype(jnp.bfloat16)

        def ffn(n):
            rows = n * MOE_SUB
            x = xs_ref[:rows, :]
            gate = _dot(x, wg_ref[...].astype(jnp.bfloat16)) + bg_ref[...]
            up = _dot(x, wu_ref[...].astype(jnp.bfloat16)) + bu_ref[...]
            gate = jnp.minimum(gate, SWIGLU_LIMIT)
            up = jnp.clip(up, -SWIGLU_LIMIT, SWIGLU_LIMIT)
            act = ((up + 1.0) * (gate * _sigmoid(gate * SWIGLU_ALPHA))).astype(jnp.bfloat16)

            @pl.when(f == 0)
            def _():
                @pl.when(t > 0)
                def _():
                    for j in range(MOE_NSUB):
                        @pl.when(j < ns_ref[t - 1])
                        def _():
                            out_copy(sb_ref[t - 1], j).wait()

                acc_ref[:rows, :] = jnp.zeros((rows, D_MODEL), jnp.float32) + bd_ref[...]

            acc_ref[:rows, :] += _dot(act, wd_ref[...].astype(jnp.bfloat16))

            @pl.when(f == nf - 1)
            def _():
                for j in range(n):
                    out_copy(sb_ref[t], j).start()

                @pl.when(t == na_ref[0] - 1)
                def _():
                    for j in range(n):
                        out_copy(sb_ref[t], j).wait()

        for n in range(1, MOE_NSUB + 1):
            pl.when(groups == n)(functools.partial(ffn, n))


def _experts(tile_expert, tile_group0, tile_groups, tile_valid, n_active, xg, w_gate_up, b_gate_up, w_down, b_down):
    nf = D_FF // MOE_TF
    tf = MOE_TF
    last_group = MOE_ROWS // MOE_SUB - 1

    def tile(t, na):
        return jnp.minimum(t, na[0] - 1)

    def fcol(t, f, na):
        return jnp.where(t < na[0], f, nf - 1)

    def group(j):
        def index_map(t, f, te, sb, ns, nv, na):
            return (jnp.minimum(sb[tile(t, na)] + j, last_group), 0)
        return index_map

    def weight(row_block, col_block):
        def index_map(t, f, te, sb, ns, nv, na):
            fc = fcol(t, f, na)
            return (0, te[tile(t, na)], row_block(fc), col_block(fc))
        return index_map

    zero = lambda fc: 0
    same = lambda fc: fc
    upper = lambda fc: nf + fc
    return pl.pallas_call(
        _experts_kernel,
        out_shape=jax.ShapeDtypeStruct((MOE_ROWS, D_MODEL), jnp.float32),
        grid_spec=pltpu.PrefetchScalarGridSpec(
            num_scalar_prefetch=5,
            grid=(MOE_TILES, nf),
            in_specs=[pl.BlockSpec((MOE_SUB, D_MODEL), group(j)) for j in range(MOE_NSUB)] + [
                pl.BlockSpec((None, None, D_MODEL, tf), weight(zero, same)),
                pl.BlockSpec((None, None, D_MODEL, tf), weight(zero, upper)),
                pl.BlockSpec((None, None, 1, tf), weight(zero, same)),
                pl.BlockSpec((None, None, 1, tf), weight(zero, upper)),
                pl.BlockSpec((None, None, tf, D_MODEL), weight(same, zero)),
                pl.BlockSpec((None, None, 1, D_MODEL), weight(zero, zero)),
            ],
            out_specs=pl.BlockSpec(memory_space=pl.ANY),
            scratch_shapes=[pltpu.VMEM((MOE_TM, D_MODEL), jnp.float32), pltpu.VMEM((MOE_TM, D_MODEL), jnp.bfloat16),
                            pltpu.SemaphoreType.DMA(())]),
        compiler_params=_cparams(("arbitrary", "arbitrary"), vmem=EXPERTS_VMEM_LIMIT),
        name="experts",
    )(tile_expert, tile_group0, tile_groups, tile_valid, n_active, *([xg] * MOE_NSUB),
      w_gate_up, w_gate_up, b_gate_up, b_gate_up, w_down, b_down)


def _combine_kernel(pos_ref, yg_ref, w_ref, h1_ref, g_ref, o_ref, buf_ref, sem):
    tt = COMB_TT
    base = pl.program_id(0) * tt

    def copy(j, k):
        return pltpu.make_async_copy(yg_ref.at[pl.ds(pos_ref[k * SEQ + base + j], 1)],
                                     buf_ref.at[k, pl.ds(j, 1)], sem)

    def start(j, c):
        for k in range(TOP_K):
            copy(j, k).start()
        return c

    def wait(j, c):
        for k in range(TOP_K):
            copy(j, k).wait()
        return c

    lax.fori_loop(0, tt, start, 0, unroll=8)
    lax.fori_loop(0, tt, wait, 0, unroll=8)
    w = w_ref[...]
    h2 = h1_ref[...]
    for k in range(TOP_K):
        h2 = h2 + w[:, k:k + 1] * buf_ref[k]
    o_ref[...] = h2 * lax.rsqrt(jnp.mean(h2 * h2, axis=-1, keepdims=True) + NORM_EPS) * g_ref[...]


def _combine(pos_flat, yg, gate_w, h1, g):
    tt = COMB_TT
    return pl.pallas_call(
        _combine_kernel,
        out_shape=jax.ShapeDtypeStruct((SEQ, D_MODEL), jnp.float32),
        grid_spec=pltpu.PrefetchScalarGridSpec(
            num_scalar_prefetch=1,
            grid=(SEQ // tt,),
            in_specs=[pl.BlockSpec(memory_space=pl.ANY),
                      pl.BlockSpec((tt, TOP_K), lambda i, pos: (i, 0)),
                      pl.BlockSpec((tt, D_MODEL), lambda i, pos: (i, 0)),
                      pl.BlockSpec((1, D_MODEL), lambda i, pos: (0, 0))],
            out_specs=pl.BlockSpec((tt, D_MODEL), lambda i, pos: (i, 0)),
            scratch_shapes=[pltpu.VMEM((TOP_K, tt, D_MODEL), jnp.float32), pltpu.SemaphoreType.DMA(())]),
        compiler_params=_cparams(("arbitrary",)),
        name="combine",
    )(pos_flat, yg, gate_w, h1, g)


def _regroup_w_in(w):
    qa_oa = w[:, :4 * A_WIDTH]
    g0 = 4 * A_WIDTH
    gates = w[:, g0:g0 + 4 * A_HEADS]
    b0 = g0 + 4 * A_HEADS
    qb = w[:, b0:b0 + B_WIDTH]
    kv = w[:, b0 + B_WIDTH:b0 + B_WIDTH + 2 * B_KV_WIDTH]
    gm = w[:, b0 + B_WIDTH + 2 * B_KV_WIDTH:]
    main = jnp.concatenate([qa_oa, gm, qb, kv], axis=1).astype(jnp.bfloat16)
    gates = jnp.pad(gates, ((0, 0), (0, LANES - 4 * A_HEADS))).astype(jnp.bfloat16)
    return main, gates


def _tile_metadata(counts):
    as_i32 = lambda a: a.astype(jnp.int32)
    groups = (counts + MOE_SUB - 1) // MOE_SUB
    group_end = jnp.cumsum(groups)
    tiles = (groups + MOE_NSUB - 1) // MOE_NSUB
    tile_end = jnp.cumsum(tiles)
    t = jnp.arange(MOE_TILES, dtype=jnp.int32)
    expert = jnp.minimum(jnp.sum(t[:, None] >= tile_end[None, :], axis=1), N_EXPERTS - 1)
    local = t - (tile_end - tiles)[expert]
    group0 = (group_end - groups)[expert] + MOE_NSUB * local
    n_groups = jnp.clip(groups[expert] - MOE_NSUB * local, 0, MOE_NSUB)
    valid = jnp.clip(counts[expert] - MOE_TM * local, 0, MOE_TM)
    return as_i32(expert), as_i32(group0), as_i32(n_groups), as_i32(valid), as_i32(tile_end[-1:])


def kernel(x, norm_mix_g, w_in, conv_w, conv_b, b_mlstm_gates, mlstm_norm_g, q_norm_g, k_norm_g, w_proj_a,
           w_proj_b, w_out, norm_ffn_g, w_router, b_router, w_gate_up, b_gate_up, w_down, b_down, norm_final_g):
    assert x.shape == (1, SEQ, D_MODEL) and w_in.shape[0] == 1
    x2 = x[0]
    w_main, w_gates = _regroup_w_in(w_in[0])
    z, gates = _in_proj(x2, norm_mix_g, w_main, w_gates)

    (qc,) = _conv(z, conv_w[0], conv_b, 0, A_HEAD_DIM ** -0.5, False)
    kc, kct = _conv(z, conv_w[0], conv_b, A_WIDTH // CONV_C, 1.0, True)
    gate_bias = jnp.pad(b_mlstm_gates.reshape(1, 4 * A_HEADS), ((0, 0), (0, LANES - 4 * A_HEADS)))
    hf, hb = _mlstm(qc, kc, kct, z, gates, gate_bias)

    half = B_HEAD_DIM // 2
    inv_freq = ROPE_THETA ** (-jnp.arange(0, half, 2, dtype=jnp.float32) / half)
    qr, kr, vr = _attn_prep(z, q_norm_g, k_norm_g, jnp.tile(inv_freq, 4).reshape(1, LANES))
    yb = _attn(qr, kr, vr)

    mixed = _mix(hf, hb, z, mlstm_norm_g, yb, w_proj_a[0].astype(jnp.bfloat16), w_proj_b[0].astype(jnp.bfloat16))
    h1, xw, logits_t = _out_proj(mixed, w_out[0].astype(jnp.bfloat16), x2, norm_ffn_g, w_router[0].T,
                                 b_router.reshape(N_EXPERTS, 1))

    _, gate_w, pos, counts = _route(logits_t)
    pos = pos.reshape(-1)
    tile_meta = _tile_metadata(counts[:, 0])

    xg = _dispatch(pos, xw)
    yg = _experts(*tile_meta, xg, w_gate_up, b_gate_up.reshape(1, N_EXPERTS, 1, 2 * D_FF),
                  w_down, b_down.reshape(1, N_EXPERTS, 1, D_MODEL))
    out = _combine(pos, yg, gate_w.T, h1, norm_final_g.reshape(1, D_MODEL))
    return out[None]
```

```python
import functools

import jax
import jax.numpy as jnp
import numpy as np
from jax import lax
from jax.experimental import pallas as pl
from jax.experimental.pallas import tpu as pltpu

D_MODEL = 2048
SEQ = 8192
GRID_W = 64
NORM_EPS = 1e-6
A_HEADS = 4
A_HEAD_DIM = 256
A_WIDTH = A_HEADS * A_HEAD_DIM
CONV_W = 5
B_HEADS = 8
B_KV_HEADS = 2
B_HEAD_DIM = 128
B_WIDTH = B_HEADS * B_HEAD_DIM
B_KV_WIDTH = B_KV_HEADS * B_HEAD_DIM
ROPE_THETA = 10000.0
N_EXPERTS = 32
TOP_K = 4
D_FF = D_MODEL
SWIGLU_LIMIT = 7.0
SWIGLU_ALPHA = 1.702
LOG2_E = 1.4426950408889634

LANES = 128
SUBLANES = 8
VMEM_LIMIT = 56 * 1024 * 1024
EXPERTS_VMEM_LIMIT = 60 * 1024 * 1024

Z_QK = 0
Z_VA = 2048
Z_OA = 3072
Z_GA = 4096
Z_GB = 6144
Z_QB = 8192
Z_KB = 9216
Z_VB = 9472
Z_WIDTH = 9728

PROJ_TM, PROJ_TN = 1024, 512
CONV_R, CONV_C = 1024, 256
A_CHUNK = 256
PREP_TM = 512
ATT_TQ, ATT_TK = 128, 512
MIX_TM = 256
OUT_TM = 256
ROUTE_TC = 512
MOE_SUB = 256
MOE_NSUB = 4
MOE_TM = MOE_SUB * MOE_NSUB
MOE_TF = 512
MOE_ROWS = SEQ * TOP_K + N_EXPERTS * MOE_SUB
MOE_TILES = SEQ * TOP_K // MOE_TM + N_EXPERTS
DISP_TT = 512
COMB_TT = 256


def _cparams(sem, vmem=VMEM_LIMIT):
    return pltpu.CompilerParams(dimension_semantics=sem, vmem_limit_bytes=vmem)


def _sigmoid(x):
    return 1.0 / (1.0 + jnp.exp(-x))


def _dot(a, b):
    return jnp.dot(a, b, preferred_element_type=jnp.float32)


def _dot_nt(a, b):
    return lax.dot_general(a, b, (((1,), (1,)), ((), ())), preferred_element_type=jnp.float32)


def _in_proj_kernel(x_ref, g_ref, wa_ref, wm_ref, wb_ref, wg_ref, z_ref, gate_ref, u_ref, *, tiles_a, tiles_m):
    j = pl.program_id(1)

    @pl.when(j == 0)
    def _():
        x = x_ref[...]
        y = x * lax.rsqrt(jnp.mean(x * x, axis=-1, keepdims=True) + NORM_EPS) * g_ref[...]
        u = y.astype(jnp.bfloat16)
        u_ref[...] = u
        gate_ref[...] = _dot(u, wg_ref[...])

    @pl.when(j < tiles_a)
    def _():
        z_ref[...] = _dot(u_ref[...], wa_ref[...].astype(jnp.bfloat16))

    @pl.when((j >= tiles_a) & (j < tiles_a + tiles_m))
    def _():
        z_ref[...] = _dot(u_ref[...], wm_ref[...])

    @pl.when(j >= tiles_a + tiles_m)
    def _():
        z_ref[...] = _dot(u_ref[...], wb_ref[...])


def _in_proj(x, g, w_in, w_gm, w_qkv, w_gate):
    tn = PROJ_TN
    tiles_a, tiles_m, tiles_b = Z_GA // tn, (Z_QB - Z_GA) // tn, (Z_WIDTH - Z_QB) // tn
    grid = (SEQ // PROJ_TM, tiles_a + tiles_m + tiles_b)
    return pl.pallas_call(
        functools.partial(_in_proj_kernel, tiles_a=tiles_a, tiles_m=tiles_m),
        out_shape=(jax.ShapeDtypeStruct((SEQ, Z_WIDTH), jnp.float32),
                   jax.ShapeDtypeStruct((SEQ, LANES), jnp.float32)),
        grid=grid,
        in_specs=[pl.BlockSpec((PROJ_TM, D_MODEL), lambda i, j: (i, 0)),
                  pl.BlockSpec((1, D_MODEL), lambda i, j: (0, 0)),
                  pl.BlockSpec((D_MODEL, tn), lambda i, j: (0, jnp.minimum(j, tiles_a - 1))),
                  pl.BlockSpec((D_MODEL, tn), lambda i, j: (0, jnp.clip(j - tiles_a, 0, tiles_m - 1))),
                  pl.BlockSpec((D_MODEL, tn), lambda i, j: (0, jnp.clip(j - tiles_a - tiles_m, 0, tiles_b - 1))),
                  pl.BlockSpec((D_MODEL, LANES), lambda i, j: (0, 0))],
        out_specs=(pl.BlockSpec((PROJ_TM, tn), lambda i, j: (i, j)),
                   pl.BlockSpec((PROJ_TM, LANES), lambda i, j: (i, 0))),
        scratch_shapes=[pltpu.VMEM((PROJ_TM, D_MODEL), jnp.bfloat16)],
        compiler_params=_cparams(("arbitrary", "arbitrary")),
        name="in_proj",
    )(x, g, w_in, w_gm, w_qkv, w_gate)


def _conv_kernel(prev_ref, main_ref, next_ref, w_ref, b_ref, *out_refs, scale, transposed):
    i = pl.program_id(0)
    n = pl.num_programs(0)
    prev = jnp.where(i > 0, prev_ref[...], 0.0)
    nxt = jnp.where(i < n - 1, next_ref[...], 0.0)
    ext = jnp.concatenate([prev, main_ref[...], nxt], axis=0)
    rows = CONV_R + 2 * SUBLANES
    w = w_ref[...]
    acc = jnp.zeros((CONV_R, CONV_C), jnp.float32) + b_ref[...]
    for j in range(CONV_W):
        shift = (CONV_W // 2 - j) % rows
        xs = ext if shift == 0 else pltpu.roll(ext, shift, axis=0)
        acc = acc + xs[SUBLANES:SUBLANES + CONV_R, :] * w[j:j + 1, :]
    y = acc * _sigmoid(acc) * scale
    out_refs[0][...] = y.astype(jnp.bfloat16)
    if transposed:
        out_refs[1][...] = y.T.astype(jnp.bfloat16)


def _conv(z, conv_w, conv_b, col_block0, scale, transposed):
    nr, ncb = SEQ // CONV_R, A_WIDTH // CONV_C
    rb = CONV_R // SUBLANES
    last = SEQ // SUBLANES - 1
    out_shape = [jax.ShapeDtypeStruct((SEQ, A_WIDTH), jnp.bfloat16)]
    out_specs = [pl.BlockSpec((CONV_R, CONV_C), lambda i, c: (i, c))]
    if transposed:
        out_shape.append(jax.ShapeDtypeStruct((A_WIDTH, SEQ), jnp.bfloat16))
        out_specs.append(pl.BlockSpec((CONV_C, CONV_R), lambda i, c: (c, i)))
    return pl.pallas_call(
        functools.partial(_conv_kernel, scale=scale, transposed=transposed),
        out_shape=tuple(out_shape),
        grid=(nr, ncb),
        in_specs=[pl.BlockSpec((SUBLANES, CONV_C), lambda i, c: (jnp.maximum(i * rb - 1, 0), col_block0 + c)),
                  pl.BlockSpec((CONV_R, CONV_C), lambda i, c: (i, col_block0 + c)),
                  pl.BlockSpec((SUBLANES, CONV_C), lambda i, c: (jnp.minimum((i + 1) * rb, last), col_block0 + c)),
                  pl.BlockSpec((CONV_W, CONV_C), lambda i, c: (0, col_block0 + c)),
                  pl.BlockSpec((1, CONV_C), lambda i, c: (0, col_block0 + c))],
        out_specs=tuple(out_specs),
        compiler_params=_cparams(("arbitrary", "arbitrary")),
        name="conv_k" if transposed else "conv_q",
    )(z, z, z, conv_w, conv_b)


def _log_sigmoid(x):
    return jnp.minimum(x, 0.0) - jnp.log(1.0 + jnp.exp(-jnp.abs(x)))


def _cumsum_rows(x):
    n = x.shape[0]
    row = lax.broadcasted_iota(jnp.int32, x.shape, 0)
    s = 1
    while s < n:
        x = x + jnp.where(row >= s, pltpu.roll(x, s, axis=0), 0.0)
        s *= 2
    return x


def _mlstm_gate_tables(g_ref, bias_ref, backward):
    L = A_CHUNK
    pre = g_ref[...] + bias_ref[...]
    logf = _log_sigmoid(pre)
    pref = _cumsum_rows(logf)
    total = pref[L - 1:L, :]
    cum = (total - pref + logf) if backward else pref
    lane = lax.broadcasted_iota(jnp.int32, (L, LANES), 1)
    f0 = (3 if backward else 1) * A_HEADS
    tab = jnp.where((lane >= f0) & (lane < f0 + A_HEADS), cum, pre)
    return tab, tab.T, total


def _mlstm_chain(q_ref, k_ref, kt_ref, v_ref, tables, h_ref, c_ref, n_ref, m_ref, h, backward):
    L, dh = A_CHUNK, A_HEAD_DIM
    tab, tab_t, total = tables
    ci = (2 if backward else 0) * A_HEADS + h
    cf = (3 if backward else 1) * A_HEADS + h
    i_c, b_c = tab[:, ci:ci + 1], tab[:, cf:cf + 1]
    i_r, b_r = tab_t[ci:ci + 1, :], tab_t[cf:cf + 1, :]
    b_last = total[:, cf:cf + 1]
    hs = slice(h * dh, (h + 1) * dh)
    q_ref, k_ref, v_ref, h_ref = q_ref.at[:, hs], k_ref.at[:, hs], v_ref.at[:, hs], h_ref.at[:, hs]
    kt_ref, c_ref, n_ref, m_ref = kt_ref.at[hs, :], c_ref.at[h], n_ref.at[h], m_ref.at[h]
    q = q_ref[...]
    k = k_ref[...]
    v = v_ref[...].astype(jnp.bfloat16)
    t_idx = lax.broadcasted_iota(jnp.int32, (L, L), 0)
    s_idx = lax.broadcasted_iota(jnp.int32, (L, L), 1)
    visible = (s_idx >= t_idx) if backward else (s_idx <= t_idx)
    d = jnp.where(visible, b_c - b_r + i_r, -jnp.inf)
    m_prev = m_ref[0:1, 0:1]
    m_inter = b_c + m_prev
    m_t = jnp.maximum(m_inter, jnp.max(d, axis=1, keepdims=True))
    w = jnp.exp(d - m_t) * _dot_nt(q, k)
    s_inter = jnp.exp(m_inter - m_t)
    c_old = c_ref[...]
    n_old = n_ref[0:1, :]
    num = _dot(w.astype(jnp.bfloat16), v) + s_inter * _dot(q, c_old.astype(jnp.bfloat16))
    qn = jnp.sum(q.astype(jnp.float32) * n_old, axis=1, keepdims=True)
    den = jnp.sum(w, axis=1, keepdims=True) + s_inter * qn
    h_ref[...] = num / jnp.maximum(jnp.abs(den), jnp.exp(-m_t))

    g_c = b_last - b_c + i_c
    g_r = b_last - b_r + i_r
    m_new = jnp.maximum(b_last + m_prev, jnp.max(g_r, axis=1, keepdims=True))
    wk_c = jnp.exp(g_c - m_new)
    wk_r = jnp.exp(g_r - m_new)
    decay = jnp.exp(b_last + m_prev - m_new)
    ktw = (kt_ref[...].astype(jnp.float32) * wk_r).astype(jnp.bfloat16)
    c_ref[...] = decay * c_old + _dot(ktw, v)
    n_new = decay * n_old + jnp.sum(k.astype(jnp.float32) * wk_c, axis=0, keepdims=True)
    n_ref[...] = jnp.broadcast_to(n_new, n_ref.shape)
    m_ref[...] = jnp.broadcast_to(m_new, m_ref.shape)


def _mlstm_kernel(qf, kf, ktf, vf, gf, qb, kb, ktb, vb, gb, bias_ref, hf_ref, hb_ref,
                  cf_ref, nf_ref, mf_ref, cb_ref, nb_ref, mb_ref):
    @pl.when(pl.program_id(0) == 0)
    def _():
        for r in (cf_ref, nf_ref, mf_ref, cb_ref, nb_ref, mb_ref):
            r[...] = jnp.zeros(r.shape, r.dtype)

    tf = _mlstm_gate_tables(gf, bias_ref, False)
    tb = _mlstm_gate_tables(gb, bias_ref, True)
    for h in range(A_HEADS):
        _mlstm_chain(qf, kf, ktf, vf, tf, hf_ref, cf_ref, nf_ref, mf_ref, h, False)
        _mlstm_chain(qb, kb, ktb, vb, tb, hb_ref, cb_ref, nb_ref, mb_ref, h, True)


def _mlstm(qc, kc, kct, z, gates, bias):
    L, dh = A_CHUNK, A_HEAD_DIM
    nc = SEQ // L

    def specs(chunk):
        return [pl.BlockSpec((L, A_WIDTH), lambda j: (chunk(j), 0)),
                pl.BlockSpec((L, A_WIDTH), lambda j: (chunk(j), 0)),
                pl.BlockSpec((A_WIDTH, L), lambda j: (0, chunk(j))),
                pl.BlockSpec((L, A_WIDTH), lambda j: (chunk(j), Z_VA // A_WIDTH)),
                pl.BlockSpec((L, LANES), lambda j: (chunk(j), 0))]

    fwd = lambda j: j
    bwd = lambda j: nc - 1 - j
    state = [pltpu.VMEM((A_HEADS, dh, dh), jnp.float32), pltpu.VMEM((A_HEADS, SUBLANES, dh), jnp.float32),
             pltpu.VMEM((A_HEADS, SUBLANES, LANES), jnp.float32)]
    return pl.pallas_call(
        _mlstm_kernel,
        out_shape=(jax.ShapeDtypeStruct((SEQ, A_WIDTH), jnp.float32),
                   jax.ShapeDtypeStruct((SEQ, A_WIDTH), jnp.float32)),
        grid=(nc,),
        in_specs=specs(fwd) + specs(bwd) + [pl.BlockSpec((1, LANES), lambda j: (0, 0))],
        out_specs=(pl.BlockSpec((L, A_WIDTH), lambda j: (fwd(j), 0)),
                   pl.BlockSpec((L, A_WIDTH), lambda j: (bwd(j), 0))),
        scratch_shapes=state + state,
        compiler_params=_cparams(("arbitrary",)),
        name="mlstm",
    )(qc, kc, kct, z, gates, qc, kc, kct, z, gates, bias)


def _attn_prep_kernel(q_ref, k_ref, v_ref, qg_ref, kg_ref, invf_ref, qo_ref, ko_ref, vo_ref):
    tm = q_ref.shape[0]
    t = pl.program_id(0) * tm + lax.broadcasted_iota(jnp.int32, (tm, LANES), 0)
    lane = lax.broadcasted_iota(jnp.int32, (tm, LANES), 1)
    half = B_HEAD_DIM // 2
    grid_shift = GRID_W.bit_length() - 1
    pos = jnp.where(lane < half, t >> grid_shift, t & (GRID_W - 1)).astype(jnp.float32)
    ang = pos * invf_ref[...]
    cos = jnp.cos(ang)
    first = (lane & (half - 1)) < (half // 2)
    sin = jnp.where(first, -jnp.sin(ang), jnp.sin(ang))

    def norm_rope(x, g, scale):
        y = x * lax.rsqrt(jnp.mean(x * x, axis=-1, keepdims=True) + NORM_EPS) * g
        partner = jnp.where(first, pltpu.roll(y, LANES - half // 2, axis=1), pltpu.roll(y, half // 2, axis=1))
        return ((y * cos + partner * sin) * scale).astype(jnp.bfloat16)

    for hd in range(B_HEADS):
        sl = slice(hd * B_HEAD_DIM, (hd + 1) * B_HEAD_DIM)
        qo_ref[:, sl] = norm_rope(q_ref[:, sl], qg_ref[...], B_HEAD_DIM ** -0.5 * LOG2_E)
    ones_col = jnp.where(lane == 0, 1.0, 0.0).astype(jnp.bfloat16)
    for hd in range(B_KV_HEADS):
        sl = slice(hd * B_HEAD_DIM, (hd + 1) * B_HEAD_DIM)
        ko_ref[:, sl] = norm_rope(k_ref[:, sl], kg_ref[...], 1.0)
        vo_ref[:, 2 * hd * B_HEAD_DIM:(2 * hd + 1) * B_HEAD_DIM] = v_ref[:, sl].astype(jnp.bfloat16)
        vo_ref[:, (2 * hd + 1) * B_HEAD_DIM:(2 * hd + 2) * B_HEAD_DIM] = ones_col


def _attn_prep(z, q_norm_g, k_norm_g, inv_freq):
    tm = PREP_TM
    return pl.pallas_call(
        _attn_prep_kernel,
        out_shape=(jax.ShapeDtypeStruct((SEQ, B_WIDTH), jnp.bfloat16),
                   jax.ShapeDtypeStruct((SEQ, B_KV_WIDTH), jnp.bfloat16),
                   jax.ShapeDtypeStruct((SEQ, 2 * B_KV_WIDTH), jnp.bfloat16)),
        grid=(SEQ // tm,),
        in_specs=[pl.BlockSpec((tm, B_WIDTH), lambda i: (i, Z_QB // B_WIDTH)),
                  pl.BlockSpec((tm, B_KV_WIDTH), lambda i: (i, Z_KB // B_KV_WIDTH)),
                  pl.BlockSpec((tm, B_KV_WIDTH), lambda i: (i, Z_VB // B_KV_WIDTH)),
                  pl.BlockSpec((1, LANES), lambda i: (0, 0)),
                  pl.BlockSpec((1, LANES), lambda i: (0, 0)),
                  pl.BlockSpec((1, LANES), lambda i: (0, 0))],
        out_specs=(pl.BlockSpec((tm, B_WIDTH), lambda i: (i, 0)),
                   pl.BlockSpec((tm, B_KV_WIDTH), lambda i: (i, 0)),
                   pl.BlockSpec((tm, 2 * B_KV_WIDTH), lambda i: (i, 0))),
        compiler_params=_cparams(("arbitrary",)),
        name="attn_prep",
    )(z, z, z, q_norm_g, k_norm_g, inv_freq)


def _attn_kernel(q_ref, k_ref, v_ref, o_ref, s_ref, m_ref, acc_ref):
    group = B_HEADS // B_KV_HEADS
    tiles = ATT_TK // LANES
    q = jnp.concatenate([q_ref[:, g * B_HEAD_DIM:(g + 1) * B_HEAD_DIM] for g in range(group)], axis=0)
    m_ref[...] = jnp.full(m_ref.shape, -jnp.inf, jnp.float32)
    acc_ref[...] = jnp.zeros(acc_ref.shape, jnp.float32)

    def scores(c, carry):
        off = pl.multiple_of(c * ATT_TK, ATT_TK)
        s = _dot_nt(q, k_ref[pl.ds(off, ATT_TK), :])
        s_ref[:, pl.ds(off, ATT_TK)] = s
        part = s[:, :LANES]
        for j in range(1, tiles):
            part = jnp.maximum(part, s[:, j * LANES:(j + 1) * LANES])
        m_ref[...] = jnp.maximum(m_ref[...], part)
        return carry

    lax.fori_loop(0, SEQ // ATT_TK, scores, 0, unroll=8)
    row_max = jnp.max(m_ref[...], axis=1, keepdims=True)
    m_ref[...] = jnp.broadcast_to(row_max, m_ref.shape)

    def weighted(c, carry):
        off = pl.multiple_of(c * ATT_TK, ATT_TK)
        m = jnp.concatenate([m_ref[...]] * tiles, axis=1)
        p = jnp.exp2(s_ref[:, pl.ds(off, ATT_TK)] - m).astype(jnp.bfloat16)
        acc_ref[...] += _dot(p, v_ref[pl.ds(off, ATT_TK), :])
        return carry

    lax.fori_loop(0, SEQ // ATT_TK, weighted, 0, unroll=8)
    out = acc_ref[:, :B_HEAD_DIM] / acc_ref[:, B_HEAD_DIM:B_HEAD_DIM + 1]
    for g in range(group):
        o_ref[:, g * B_HEAD_DIM:(g + 1) * B_HEAD_DIM] = out[g * ATT_TQ:(g + 1) * ATT_TQ].astype(o_ref.dtype)


def _attn(qr, kr, vr):
    group = B_HEADS // B_KV_HEADS
    rows = group * ATT_TQ
    return pl.pallas_call(
        _attn_kernel,
        out_shape=jax.ShapeDtypeStruct((SEQ, B_WIDTH), jnp.bfloat16),
        grid=(B_KV_HEADS, SEQ // ATT_TQ),
        in_specs=[pl.BlockSpec((ATT_TQ, group * B_HEAD_DIM), lambda kv, i: (i, kv)),
                  pl.BlockSpec((SEQ, B_HEAD_DIM), lambda kv, i: (0, kv)),
                  pl.BlockSpec((SEQ, 2 * B_HEAD_DIM), lambda kv, i: (0, kv))],
        out_specs=pl.BlockSpec((ATT_TQ, group * B_HEAD_DIM), lambda kv, i: (i, kv)),
        scratch_shapes=[pltpu.VMEM((rows, SEQ), jnp.float32), pltpu.VMEM((rows, LANES), jnp.float32),
                        pltpu.VMEM((rows, 2 * B_HEAD_DIM), jnp.float32)],
        compiler_params=_cparams(("arbitrary", "arbitrary")),
        name="attn",
    )(qr, kr, vr)


def _mix_kernel(hf_ref, hb_ref, oa_ref, ng_ref, yb_ref, ga_ref, gb_ref, pa_ref, pb_ref, o_ref):
    parts = []
    for hd in range(A_HEADS):
        sl = slice(hd * A_HEAD_DIM, (hd + 1) * A_HEAD_DIM)
        hs = hf_ref[:, sl] + hb_ref[:, sl]
        y = hs * lax.rsqrt(jnp.mean(hs * hs, axis=-1, keepdims=True) + NORM_EPS) * ng_ref[:, sl]
        parts.append((_sigmoid(oa_ref[:, sl]) * y).astype(jnp.bfloat16))
    ya = jnp.concatenate(parts, axis=1)
    ta = _dot(ya, pa_ref[...])
    tb = _dot(yb_ref[...], pb_ref[...])
    o_ref[...] = (_sigmoid(ga_ref[...]) * ta + _sigmoid(gb_ref[...]) * tb).astype(o_ref.dtype)


def _mix(hf, hb, z, norm_g, yb, pa, pb):
    tm = MIX_TM
    return pl.pallas_call(
        _mix_kernel,
        out_shape=jax.ShapeDtypeStruct((SEQ, D_MODEL), jnp.bfloat16),
        grid=(SEQ // tm,),
        in_specs=[pl.BlockSpec((tm, A_WIDTH), lambda i: (i, 0)),
                  pl.BlockSpec((tm, A_WIDTH), lambda i: (i, 0)),
                  pl.BlockSpec((tm, A_WIDTH), lambda i: (i, Z_OA // A_WIDTH)),
                  pl.BlockSpec((1, A_WIDTH), lambda i: (0, 0)),
                  pl.BlockSpec((tm, B_WIDTH), lambda i: (i, 0)),
                  pl.BlockSpec((tm, D_MODEL), lambda i: (i, Z_GA // D_MODEL)),
                  pl.BlockSpec((tm, D_MODEL), lambda i: (i, Z_GB // D_MODEL)),
                  pl.BlockSpec((A_WIDTH, D_MODEL), lambda i: (0, 0)),
                  pl.BlockSpec((B_WIDTH, D_MODEL), lambda i: (0, 0))],
        out_specs=pl.BlockSpec((tm, D_MODEL), lambda i: (i, 0)),
        compiler_params=_cparams(("arbitrary",)),
        name="mix",
    )(hf, hb, z, norm_g, yb, z, z, pa, pb)


def _split_bf16(a):
    hi = a.astype(jnp.bfloat16)
    lo = (a - hi.astype(jnp.float32)).astype(jnp.bfloat16)
    return hi, lo


def _out_proj_kernel(mx_ref, w_ref, x_ref, g_ref, wr_ref, br_ref, h1_ref, xn_ref, lg_ref):
    h1 = x_ref[...] + _dot(mx_ref[...], w_ref[...])
    h1_ref[...] = h1
    xn = h1 * lax.rsqrt(jnp.mean(h1 * h1, axis=-1, keepdims=True) + NORM_EPS) * g_ref[...]
    xn_ref[...] = xn
    x_hi, x_lo = _split_bf16(xn)
    w_hi, w_lo = _split_bf16(wr_ref[...])
    logits = _dot(x_hi, w_hi) + (_dot(x_lo, w_hi) + _dot(x_hi, w_lo)) + br_ref[...]
    lg_ref[...] = logits.T[:N_EXPERTS, :]


def _out_proj(mixed, w_out, x, g, w_router, b_router):
    tm = OUT_TM
    return pl.pallas_call(
        _out_proj_kernel,
        out_shape=(jax.ShapeDtypeStruct((SEQ, D_MODEL), jnp.float32),
                   jax.ShapeDtypeStruct((SEQ, D_MODEL), jnp.float32),
                   jax.ShapeDtypeStruct((N_EXPERTS, SEQ), jnp.float32)),
        grid=(SEQ // tm,),
        in_specs=[pl.BlockSpec((tm, D_MODEL), lambda i: (i, 0)),
                  pl.BlockSpec((D_MODEL, D_MODEL), lambda i: (0, 0)),
                  pl.BlockSpec((tm, D_MODEL), lambda i: (i, 0)),
                  pl.BlockSpec((1, D_MODEL), lambda i: (0, 0)),
                  pl.BlockSpec((D_MODEL, LANES), lambda i: (0, 0)),
                  pl.BlockSpec((1, LANES), lambda i: (0, 0))],
        out_specs=(pl.BlockSpec((tm, D_MODEL), lambda i: (i, 0)),
                   pl.BlockSpec((tm, D_MODEL), lambda i: (i, 0)),
                   pl.BlockSpec((N_EXPERTS, tm), lambda i: (0, i))),
        compiler_params=_cparams(("arbitrary",)),
        name="out_proj",
    )(mixed, w_out, x, g, w_router, b_router)


def _route_kernel(lg_ref, e_ref, w_ref, pos_ref, cnt_ref, r_ref, carry_ref):
    tc = ROUTE_TC
    carry_ref[...] = jnp.zeros(carry_ref.shape, jnp.float32)
    eidx = lax.broadcasted_iota(jnp.int32, (N_EXPERTS, tc), 0)
    tri_r = lax.broadcasted_iota(jnp.int32, (tc, tc), 0)
    tri_c = lax.broadcasted_iota(jnp.int32, (tc, tc), 1)
    before = jnp.where(tri_r < tri_c, 1.0, 0.0).astype(jnp.bfloat16)

    def body(c, carry):
        off = pl.multiple_of(c * tc, tc)
        lg = lg_ref[:, pl.ds(off, tc)]
        vals, sels = [], []
        for _ in range(TOP_K):
            mx = jnp.max(lg, axis=0, keepdims=True)
            idx = jnp.min(jnp.where(lg == mx, eidx, N_EXPERTS), axis=0, keepdims=True)
            sel = eidx == idx
            vals.append(mx)
            sels.append(sel)
            lg = jnp.where(sel, -jnp.inf, lg)
        ex = [jnp.exp(v - vals[0]) for v in vals]
        tot = ex[0] + ex[1] + ex[2] + ex[3]
        chosen = jnp.where(sels[0] | sels[1] | sels[2] | sels[3], 1.0, 0.0)
        rank = _dot(chosen.astype(jnp.bfloat16), before) + carry_ref[:, 0:1]
        carry_ref[...] = carry_ref[...] + jnp.sum(chosen, axis=1, keepdims=True)
        for k in range(TOP_K):
            e_ref[k:k + 1, pl.ds(off, tc)] = jnp.sum(jnp.where(sels[k], eidx, 0), axis=0, keepdims=True)
            w_ref[k:k + 1, pl.ds(off, tc)] = ex[k] / tot
            r_ref[k:k + 1, pl.ds(off, tc)] = jnp.sum(jnp.where(sels[k], rank, 0.0), axis=0, keepdims=True)
        return carry

    lax.fori_loop(0, SEQ // tc, body, 0)

    counts = carry_ref[...].astype(jnp.int32)
    cnt_ref[...] = counts
    sub_shift = MOE_SUB.bit_length() - 1
    padded = (((counts + (MOE_SUB - 1)) >> sub_shift) << sub_shift).astype(jnp.float32)
    start = (_cumsum_rows(padded) - padded)[:, 0:1]

    def place(c, carry):
        off = pl.multiple_of(c * tc, tc)
        for k in range(TOP_K):
            sel = eidx == e_ref[k:k + 1, pl.ds(off, tc)]
            base = jnp.sum(jnp.where(sel, start, 0.0), axis=0, keepdims=True)
            pos_ref[k:k + 1, pl.ds(off, tc)] = (base + r_ref[k:k + 1, pl.ds(off, tc)]).astype(jnp.int32)
        return carry

    lax.fori_loop(0, SEQ // tc, place, 0)


def _route(logits_t):
    return pl.pallas_call(
        _route_kernel,
        out_shape=(jax.ShapeDtypeStruct((TOP_K, SEQ), jnp.int32),
                   jax.ShapeDtypeStruct((TOP_K, SEQ), jnp.float32),
                   jax.ShapeDtypeStruct((TOP_K, SEQ), jnp.int32),
                   jax.ShapeDtypeStruct((N_EXPERTS, LANES), jnp.int32)),
        scratch_shapes=[pltpu.VMEM((TOP_K, SEQ), jnp.float32), pltpu.VMEM((N_EXPERTS, LANES), jnp.float32)],
        compiler_params=_cparams(None),
        name="route",
    )(logits_t)


def _dispatch_kernel(pos_ref, xw_ref, xg_ref, sem):
    base = pl.program_id(0) * DISP_TT

    def copy(j, k):
        return pltpu.make_async_copy(xw_ref.at[pl.ds(j, 1)], xg_ref.at[pl.ds(pos_ref[k * SEQ + base + j], 1)], sem)

    def start(j, c):
        for k in range(TOP_K):
            copy(j, k).start()
        return c

    def wait(j, c):
        for k in range(TOP_K):
            copy(j, k).wait()
        return c

    lax.fori_loop(0, DISP_TT, start, 0, unroll=8)
    lax.fori_loop(0, DISP_TT, wait, 0, unroll=8)


def _dispatch(pos_flat, xw):
    return pl.pallas_call(
        _dispatch_kernel,
        out_shape=jax.ShapeDtypeStruct((MOE_ROWS, D_MODEL), jnp.float32),
        grid_spec=pltpu.PrefetchScalarGridSpec(
            num_scalar_prefetch=1,
            grid=(SEQ // DISP_TT,),
            in_specs=[pl.BlockSpec((DISP_TT, D_MODEL), lambda i, pos: (i, 0))],
            out_specs=pl.BlockSpec(memory_space=pl.ANY),
            scratch_shapes=[pltpu.SemaphoreType.DMA(())]),
        compiler_params=_cparams(("arbitrary",)),
        name="dispatch",
    )(pos_flat, xw)


def _experts_kernel(te_ref, sb_ref, ns_ref, nv_ref, na_ref, *refs):
    x_refs = refs[:MOE_NSUB]
    wg_ref, wu_ref, bg_ref, bu_ref, wd_ref, bd_ref, yg_ref, acc_ref, xs_ref, sem = refs[MOE_NSUB:]
    t = pl.program_id(0)
    f = pl.program_id(1)
    nf = pl.num_programs(1)

    def out_copy(first_group, j):
        rows = pl.ds(pl.multiple_of((first_group + j) * MOE_SUB, MOE_SUB), MOE_SUB)
        return pltpu.make_async_copy(acc_ref.at[pl.ds(j * MOE_SUB, MOE_SUB)], yg_ref.at[rows], sem)

    @pl.when(t < na_ref[0])
    def _():
        groups = ns_ref[t]
        valid = nv_ref[t]

        @pl.when(f == 0)
        def _():
            for j, x_ref in enumerate(x_refs):
                row = j * MOE_SUB + lax.broadcasted_iota(jnp.int32, x_ref.shape, 0)
                xs_ref[j * MOE_SUB:(j + 1) * MOE_SUB, :] = jnp.where(row < valid, x_ref[...], 0.0).astype(jnp.bfloat16)

        def ffn(n):
            rows = n * MOE_SUB
            x = xs_ref[:rows, :]
            gate = _dot(x, wg_ref[...].astype(jnp.bfloat16)) + bg_ref[...]
            up = _dot(x, wu_ref[...].astype(jnp.bfloat16)) + bu_ref[...]
            gate = jnp.minimum(gate, SWIGLU_LIMIT)
            up = jnp.clip(up, -SWIGLU_LIMIT, SWIGLU_LIMIT)
            act = ((up + 1.0) * (gate * _sigmoid(gate * SWIGLU_ALPHA))).astype(jnp.bfloat16)

            @pl.when(f == 0)
            def _():
                @pl.when(t > 0)
                def _():
                    for j in range(MOE_NSUB):
                        @pl.when(j < ns_ref[t - 1])
                        def _():
                            out_copy(sb_ref[t - 1], j).wait()

                acc_ref[:rows, :] = jnp.zeros((rows, D_MODEL), jnp.float32) + bd_ref[...]

            acc_ref[:rows, :] += _dot(act, wd_ref[...].astype(jnp.bfloat16))

            @pl.when(f == nf - 1)
            def _():
                for j in range(n):
                    out_copy(sb_ref[t], j).start()

                @pl.when(t == na_ref[0] - 1)
                def _():
                    for j in range(n):
                        out_copy(sb_ref[t], j).wait()

        for n in range(1, MOE_NSUB + 1):
            pl.when(groups == n)(functools.partial(ffn, n))


def _experts(tile_expert, tile_group0, tile_groups, tile_valid, n_active, xg, w_gate_up, b_gate_up, w_down, b_down):
    nf = D_FF // MOE_TF
    tf = MOE_TF
    last_group = MOE_ROWS // MOE_SUB - 1

    def tile(t, na):
        return jnp.minimum(t, na[0] - 1)

    def fcol(t, f, na):
        return jnp.where(t < na[0], f, nf - 1)

    def group(j):
        def index_map(t, f, te, sb, ns, nv, na):
            return (jnp.minimum(sb[tile(t, na)] + j, last_group), 0)
        return index_map

    def weight(row_block, col_block):
        def index_map(t, f, te, sb, ns, nv, na):
            fc = fcol(t, f, na)
            return (0, te[tile(t, na)], row_block(fc), col_block(fc))
        return index_map

    zero = lambda fc: 0
    same = lambda fc: fc
    upper = lambda fc: nf + fc
    return pl.pallas_call(
        _experts_kernel,
        out_shape=jax.ShapeDtypeStruct((MOE_ROWS, D_MODEL), jnp.float32),
        grid_spec=pltpu.PrefetchScalarGridSpec(
            num_scalar_prefetch=5,
            grid=(MOE_TILES, nf),
            in_specs=[pl.BlockSpec((MOE_SUB, D_MODEL), group(j)) for j in range(MOE_NSUB)] + [
                pl.BlockSpec((None, None, D_MODEL, tf), weight(zero, same)),
                pl.BlockSpec((None, None, D_MODEL, tf), weight(zero, upper)),
                pl.BlockSpec((None, None, 1, tf), weight(zero, same)),
                pl.BlockSpec((None, None, 1, tf), weight(zero, upper)),
                pl.BlockSpec((None, None, tf, D_MODEL), weight(same, zero)),
                pl.BlockSpec((None, None, 1, D_MODEL), weight(zero, zero)),
            ],
            out_specs=pl.BlockSpec(memory_space=pl.ANY),
            scratch_shapes=[pltpu.VMEM((MOE_TM, D_MODEL), jnp.float32), pltpu.VMEM((MOE_TM, D_MODEL), jnp.bfloat16),
                            pltpu.SemaphoreType.DMA(())]),
        compiler_params=_cparams(("arbitrary", "arbitrary"), vmem=EXPERTS_VMEM_LIMIT),
        name="experts",
    )(tile_expert, tile_group0, tile_groups, tile_valid, n_active, *([xg] * MOE_NSUB),
      w_gate_up, w_gate_up, b_gate_up, b_gate_up, w_down, b_down)


def _combine_kernel(pos_ref, yg_ref, w_ref, h1_ref, g_ref, o_ref, buf_ref, sem):
    tt = COMB_TT
    base = pl.program_id(0) * tt

    def copy(j, k):
        return pltpu.make_async_copy(yg_ref.at[pl.ds(pos_ref[k * SEQ + base + j], 1)],
                                     buf_ref.at[k, pl.ds(j, 1)], sem)

    def start(j, c):
        for k in range(TOP_K):
            copy(j, k).start()
        return c

    def wait(j, c):
        for k in range(TOP_K):
            copy(j, k).wait()
        return c

    lax.fori_loop(0, tt, start, 0, unroll=8)
    lax.fori_loop(0, tt, wait, 0, unroll=8)
    w = w_ref[...]
    h2 = h1_ref[...]
    for k in range(TOP_K):
        h2 = h2 + w[:, k:k + 1] * buf_ref[k]
    o_ref[...] = h2 * lax.rsqrt(jnp.mean(h2 * h2, axis=-1, keepdims=True) + NORM_EPS) * g_ref[...]


def _combine(pos_flat, yg, gate_w, h1, g):
    tt = COMB_TT
    return pl.pallas_call(
        _combine_kernel,
        out_shape=jax.ShapeDtypeStruct((SEQ, D_MODEL), jnp.float32),
        grid_spec=pltpu.PrefetchScalarGridSpec(
            num_scalar_prefetch=1,
            grid=(SEQ // tt,),
            in_specs=[pl.BlockSpec(memory_space=pl.ANY),
                      pl.BlockSpec((tt, TOP_K), lambda i, pos: (i, 0)),
                      pl.BlockSpec((tt, D_MODEL), lambda i, pos: (i, 0)),
                      pl.BlockSpec((1, D_MODEL), lambda i, pos: (0, 0))],
            out_specs=pl.BlockSpec((tt, D_MODEL), lambda i, pos: (i, 0)),
            scratch_shapes=[pltpu.VMEM((TOP_K, tt, D_MODEL), jnp.float32), pltpu.SemaphoreType.DMA(())]),
        compiler_params=_cparams(("arbitrary",)),
        name="combine",
    )(pos_flat, yg, gate_w, h1, g)


def _regroup_w_in(w):
    g0 = 4 * A_WIDTH
    b0 = g0 + 4 * A_HEADS
    m0 = b0 + B_WIDTH + 2 * B_KV_WIDTH
    gm = w[:, m0:].astype(jnp.bfloat16)
    qkv = w[:, b0:m0].astype(jnp.bfloat16)
    gates = jnp.pad(w[:, g0:b0], ((0, 0), (0, LANES - 4 * A_HEADS))).astype(jnp.bfloat16)
    return gm, qkv, gates


def _tile_metadata(counts):
    as_i32 = lambda a: a.astype(jnp.int32)
    groups = (counts + MOE_SUB - 1) // MOE_SUB
    group_end = jnp.cumsum(groups)
    tiles = (groups + MOE_NSUB - 1) // MOE_NSUB
    tile_end = jnp.cumsum(tiles)
    t = jnp.arange(MOE_TILES, dtype=jnp.int32)
    expert = jnp.minimum(jnp.sum(t[:, None] >= tile_end[None, :], axis=1), N_EXPERTS - 1)
    local = t - (tile_end - tiles)[expert]
    group0 = (group_end - groups)[expert] + MOE_NSUB * local
    n_groups = jnp.clip(groups[expert] - MOE_NSUB * local, 0, MOE_NSUB)
    valid = jnp.clip(counts[expert] - MOE_TM * local, 0, MOE_TM)
    return as_i32(expert), as_i32(group0), as_i32(n_groups), as_i32(valid), as_i32(tile_end[-1:])


def kernel(x, norm_mix_g, w_in, conv_w, conv_b, b_mlstm_gates, mlstm_norm_g, q_norm_g, k_norm_g, w_proj_a,
           w_proj_b, w_out, norm_ffn_g, w_router, b_router, w_gate_up, b_gate_up, w_down, b_down, norm_final_g):
    assert x.shape == (1, SEQ, D_MODEL) and w_in.shape[0] == 1
    x2 = x[0]
    w_gm, w_qkv, w_gates = _regroup_w_in(w_in[0])
    z, gates = _in_proj(x2, norm_mix_g, w_in[0], w_gm, w_qkv, w_gates)

    (qc,) = _conv(z, conv_w[0], conv_b, 0, A_HEAD_DIM ** -0.5, False)
    kc, kct = _conv(z, conv_w[0], conv_b, A_WIDTH // CONV_C, 1.0, True)
    gate_bias = jnp.pad(b_mlstm_gates.reshape(1, 4 * A_HEADS), ((0, 0), (0, LANES - 4 * A_HEADS)))
    hf, hb = _mlstm(qc, kc, kct, z, gates, gate_bias)

    half = B_HEAD_DIM // 2
    inv_freq = ROPE_THETA ** (-jnp.arange(0, half, 2, dtype=jnp.float32) / half)
    qr, kr, vr = _attn_prep(z, q_norm_g, k_norm_g, jnp.tile(inv_freq, 4).reshape(1, LANES))
    yb = _attn(qr, kr, vr)

    mixed = _mix(hf, hb, z, mlstm_norm_g, yb, w_proj_a[0].astype(jnp.bfloat16), w_proj_b[0].astype(jnp.bfloat16))
    lane_pad = ((0, 0), (0, LANES - N_EXPERTS))
    h1, xw, logits_t = _out_proj(mixed, w_out[0].astype(jnp.bfloat16), x2, norm_ffn_g,
                                 jnp.pad(w_router[0], lane_pad), jnp.pad(b_router, lane_pad))

    _, gate_w, pos, counts = _route(logits_t)
    pos = pos.reshape(-1)
    tile_meta = _tile_metadata(counts[:, 0])

    xg = _dispatch(pos, xw)
    yg = _experts(*tile_meta, xg, w_gate_up, b_gate_up.reshape(1, N_EXPERTS, 1, 2 * D_FF),
                  w_down, b_down.reshape(1, N_EXPERTS, 1, D_MODEL))
    out = _combine(pos, yg, gate_w.T, h1, norm_final_g.reshape(1, D_MODEL))
    return out[None]
```

```python
import functools

import jax
import jax.numpy as jnp
import numpy as np
from jax import lax
from jax.experimental import pallas as pl
from jax.experimental.pallas import tpu as pltpu

D_MODEL = 2048
SEQ = 8192
GRID_W = 64
NORM_EPS = 1e-6
A_HEADS = 4
A_HEAD_DIM = 256
A_WIDTH = A_HEADS * A_HEAD_DIM
CONV_W = 5
B_HEADS = 8
B_KV_HEADS = 2
B_HEAD_DIM = 128
B_WIDTH = B_HEADS * B_HEAD_DIM
B_KV_WIDTH = B_KV_HEADS * B_HEAD_DIM
ROPE_THETA = 10000.0
N_EXPERTS = 32
TOP_K = 4
D_FF = D_MODEL
SWIGLU_LIMIT = 7.0
SWIGLU_ALPHA = 1.702
LOG2_E = 1.4426950408889634

LANES = 128
SUBLANES = 8
VMEM_LIMIT = 56 * 1024 * 1024
EXPERTS_VMEM_LIMIT = 60 * 1024 * 1024

Z_QK = 0
Z_VA = 2048
Z_OA = 3072
Z_GA = 4096
Z_GB = 6144
Z_QB = 8192
Z_KB = 9216
Z_VB = 9472
Z_WIDTH = 9728

PROJ_TM, PROJ_TN = 1024, 512
CONV_R, CONV_C = 1024, 256
A_CHUNK = 256
PREP_TM = 512
ATT_TQ, ATT_TK = 128, 512
MIX_TM = 256
OUT_TM = 256
ROUTE_TC = 512
MOE_SUB = 256
MOE_NSUB = 4
MOE_TM = MOE_SUB * MOE_NSUB
MOE_TF = 512
MOE_ROWS = SEQ * TOP_K + N_EXPERTS * MOE_SUB
MOE_TILES = SEQ * TOP_K // MOE_TM + N_EXPERTS
DISP_TT = 512
COMB_TT = 256


def _cparams(sem, vmem=VMEM_LIMIT):
    return pltpu.CompilerParams(dimension_semantics=sem, vmem_limit_bytes=vmem)


def _sigmoid(x):
    return 1.0 / (1.0 + jnp.exp(-x))


def _dot(a, b):
    return jnp.dot(a, b, preferred_element_type=jnp.float32)


def _dot_nt(a, b):
    return lax.dot_general(a, b, (((1,), (1,)), ((), ())), preferred_element_type=jnp.float32)


def _in_proj_kernel(x_ref, g_ref, w_ref, wg_ref, z_ref, gate_ref, u_ref):
    @pl.when(pl.program_id(1) == 0)
    def _():
        x = x_ref[...]
        y = x * lax.rsqrt(jnp.mean(x * x, axis=-1, keepdims=True) + NORM_EPS) * g_ref[...]
        u = y.astype(jnp.bfloat16)
        u_ref[...] = u
        gate_ref[...] = _dot(u, wg_ref[...])

    z_ref[...] = _dot(u_ref[...], w_ref[...])


def _in_proj(x, g, w_main, w_gate):
    tn = PROJ_TN
    grid = (SEQ // PROJ_TM, Z_WIDTH // tn)
    return pl.pallas_call(
        _in_proj_kernel,
        out_shape=(jax.ShapeDtypeStruct((SEQ, Z_WIDTH), jnp.float32),
                   jax.ShapeDtypeStruct((SEQ, LANES), jnp.float32)),
        grid=grid,
        in_specs=[pl.BlockSpec((PROJ_TM, D_MODEL), lambda i, j: (i, 0)),
                  pl.BlockSpec((1, D_MODEL), lambda i, j: (0, 0)),
                  pl.BlockSpec((D_MODEL, tn), lambda i, j: (0, j)),
                  pl.BlockSpec((D_MODEL, LANES), lambda i, j: (0, 0))],
        out_specs=(pl.BlockSpec((PROJ_TM, tn), lambda i, j: (i, j)),
                   pl.BlockSpec((PROJ_TM, LANES), lambda i, j: (i, 0))),
        scratch_shapes=[pltpu.VMEM((PROJ_TM, D_MODEL), jnp.bfloat16)],
        compiler_params=_cparams(("arbitrary", "arbitrary")),
        name="in_proj",
    )(x, g, w_main, w_gate)


def _conv_kernel(prev_ref, main_ref, next_ref, w_ref, b_ref, *out_refs, scale, transposed):
    i = pl.program_id(0)
    n = pl.num_programs(0)
    prev = jnp.where(i > 0, prev_ref[...], 0.0)
    nxt = jnp.where(i < n - 1, next_ref[...], 0.0)
    ext = jnp.concatenate([prev, main_ref[...], nxt], axis=0)
    rows = CONV_R + 2 * SUBLANES
    w = w_ref[...]
    acc = jnp.zeros((CONV_R, CONV_C), jnp.float32) + b_ref[...]
    for j in range(CONV_W):
        shift = (CONV_W // 2 - j) % rows
        xs = ext if shift == 0 else pltpu.roll(ext, shift, axis=0)
        acc = acc + xs[SUBLANES:SUBLANES + CONV_R, :] * w[j:j + 1, :]
    y = acc * _sigmoid(acc) * scale
    out_refs[0][...] = y.astype(jnp.bfloat16)
    if transposed:
        out_refs[1][...] = y.T.astype(jnp.bfloat16)


def _conv(z, conv_w, conv_b, col_block0, scale, transposed):
    nr, ncb = SEQ // CONV_R, A_WIDTH // CONV_C
    rb = CONV_R // SUBLANES
    last = SEQ // SUBLANES - 1
    out_shape = [jax.ShapeDtypeStruct((SEQ, A_WIDTH), jnp.bfloat16)]
    out_specs = [pl.BlockSpec((CONV_R, CONV_C), lambda i, c: (i, c))]
    if transposed:
        out_shape.append(jax.ShapeDtypeStruct((A_WIDTH, SEQ), jnp.bfloat16))
        out_specs.append(pl.BlockSpec((CONV_C, CONV_R), lambda i, c: (c, i)))
    return pl.pallas_call(
        functools.partial(_conv_kernel, scale=scale, transposed=transposed),
        out_shape=tuple(out_shape),
        grid=(nr, ncb),
        in_specs=[pl.BlockSpec((SUBLANES, CONV_C), lambda i, c: (jnp.maximum(i * rb - 1, 0), col_block0 + c)),
                  pl.BlockSpec((CONV_R, CONV_C), lambda i, c: (i, col_block0 + c)),
                  pl.BlockSpec((SUBLANES, CONV_C), lambda i, c: (jnp.minimum((i + 1) * rb, last), col_block0 + c)),
                  pl.BlockSpec((CONV_W, CONV_C), lambda i, c: (0, col_block0 + c)),
                  pl.BlockSpec((1, CONV_C), lambda i, c: (0, col_block0 + c))],
        out_specs=tuple(out_specs),
        compiler_params=_cparams(("arbitrary", "arbitrary")),
        name="conv_k" if transposed else "conv_q",
    )(z, z, z, conv_w, conv_b)


def _log_sigmoid(x):
    return jnp.minimum(x, 0.0) - jnp.log(1.0 + jnp.exp(-jnp.abs(x)))


def _cumsum_rows(x):
    n = x.shape[0]
    row = lax.broadcasted_iota(jnp.int32, x.shape, 0)
    s = 1
    while s < n:
        x = x + jnp.where(row >= s, pltpu.roll(x, s, axis=0), 0.0)
        s *= 2
    return x


def _mlstm_gate_tables(g_ref, bias_ref, backward):
    L = A_CHUNK
    pre = g_ref[...] + bias_ref[...]
    logf = _log_sigmoid(pre)
    pref = _cumsum_rows(logf)
    total = pref[L - 1:L, :]
    cum = (total - pref + logf) if backward else pref
    lane = lax.broadcasted_iota(jnp.int32, (L, LANES), 1)
    f0 = (3 if backward else 1) * A_HEADS
    tab = jnp.where((lane >= f0) & (lane < f0 + A_HEADS), cum, pre)
    return tab, tab.T, total


def _mlstm_chain(q_ref, k_ref, kt_ref, v_ref, tables, h_ref, c_ref, n_ref, m_ref, h, backward):
    L, dh = A_CHUNK, A_HEAD_DIM
    tab, tab_t, total = tables
    ci = (2 if backward else 0) * A_HEADS + h
    cf = (3 if backward else 1) * A_HEADS + h
    i_c, b_c = tab[:, ci:ci + 1], tab[:, cf:cf + 1]
    i_r, b_r = tab_t[ci:ci + 1, :], tab_t[cf:cf + 1, :]
    b_last = total[:, cf:cf + 1]
    hs = slice(h * dh, (h + 1) * dh)
    q_ref, k_ref, v_ref, h_ref = q_ref.at[:, hs], k_ref.at[:, hs], v_ref.at[:, hs], h_ref.at[:, hs]
    kt_ref, c_ref, n_ref, m_ref = kt_ref.at[hs, :], c_ref.at[h], n_ref.at[h], m_ref.at[h]
    q = q_ref[...]
    k = k_ref[...]
    v = v_ref[...].astype(jnp.bfloat16)
    t_idx = lax.broadcasted_iota(jnp.int32, (L, L), 0)
    s_idx = lax.broadcasted_iota(jnp.int32, (L, L), 1)
    visible = (s_idx >= t_idx) if backward else (s_idx <= t_idx)
    d = jnp.where(visible, b_c - b_r + i_r, -jnp.inf)
    m_prev = m_ref[0:1, 0:1]
    m_inter = b_c + m_prev
    m_t = jnp.maximum(m_inter, jnp.max(d, axis=1, keepdims=True))
    w = jnp.exp(d - m_t) * _dot_nt(q, k)
    s_inter = jnp.exp(m_inter - m_t)
    c_old = c_ref[...]
    n_old = n_ref[0:1, :]
    num = _dot(w.astype(jnp.bfloat16), v) + s_inter * _dot(q, c_old.astype(jnp.bfloat16))
    qn = jnp.sum(q.astype(jnp.float32) * n_old, axis=1, keepdims=True)
    den = jnp.sum(w, axis=1, keepdims=True) + s_inter * qn
    h_ref[...] = num / jnp.maximum(jnp.abs(den), jnp.exp(-m_t))

    g_c = b_last - b_c + i_c
    g_r = b_last - b_r + i_r
    m_new = jnp.maximum(b_last + m_prev, jnp.max(g_r, axis=1, keepdims=True))
    wk_c = jnp.exp(g_c - m_new)
    wk_r = jnp.exp(g_r - m_new)
    decay = jnp.exp(b_last + m_prev - m_new)
    ktw = (kt_ref[...].astype(jnp.float32) * wk_r).astype(jnp.bfloat16)
    c_ref[...] = decay * c_old + _dot(ktw, v)
    n_new = decay * n_old + jnp.sum(k.astype(jnp.float32) * wk_c, axis=0, keepdims=True)
    n_ref[...] = jnp.broadcast_to(n_new, n_ref.shape)
    m_ref[...] = jnp.broadcast_to(m_new, m_ref.shape)


def _mlstm_kernel(qf, kf, ktf, vf, gf, qb, kb, ktb, vb, gb, bias_ref, hf_ref, hb_ref,
                  cf_ref, nf_ref, mf_ref, cb_ref, nb_ref, mb_ref):
    @pl.when(pl.program_id(0) == 0)
    def _():
        for r in (cf_ref, nf_ref, mf_ref, cb_ref, nb_ref, mb_ref):
            r[...] = jnp.zeros(r.shape, r.dtype)

    tf = _mlstm_gate_tables(gf, bias_ref, False)
    tb = _mlstm_gate_tables(gb, bias_ref, True)
    for h in range(A_HEADS):
        _mlstm_chain(qf, kf, ktf, vf, tf, hf_ref, cf_ref, nf_ref, mf_ref, h, False)
        _mlstm_chain(qb, kb, ktb, vb, tb, hb_ref, cb_ref, nb_ref, mb_ref, h, True)


def _mlstm(qc, kc, kct, z, gates, bias):
    L, dh = A_CHUNK, A_HEAD_DIM
    nc = SEQ // L

    def specs(chunk):
        return [pl.BlockSpec((L, A_WIDTH), lambda j: (chunk(j), 0)),
                pl.BlockSpec((L, A_WIDTH), lambda j: (chunk(j), 0)),
                pl.BlockSpec((A_WIDTH, L), lambda j: (0, chunk(j))),
                pl.BlockSpec((L, A_WIDTH), lambda j: (chunk(j), Z_VA // A_WIDTH)),
                pl.BlockSpec((L, LANES), lambda j: (chunk(j), 0))]

    fwd = lambda j: j
    bwd = lambda j: nc - 1 - j
    state = [pltpu.VMEM((A_HEADS, dh, dh), jnp.float32), pltpu.VMEM((A_HEADS, SUBLANES, dh), jnp.float32),
             pltpu.VMEM((A_HEADS, SUBLANES, LANES), jnp.float32)]
    return pl.pallas_call(
        _mlstm_kernel,
        out_shape=(jax.ShapeDtypeStruct((SEQ, A_WIDTH), jnp.float32),
                   jax.ShapeDtypeStruct((SEQ, A_WIDTH), jnp.float32)),
        grid=(nc,),
        in_specs=specs(fwd) + specs(bwd) + [pl.BlockSpec((1, LANES), lambda j: (0, 0))],
        out_specs=(pl.BlockSpec((L, A_WIDTH), lambda j: (fwd(j), 0)),
                   pl.BlockSpec((L, A_WIDTH), lambda j: (bwd(j), 0))),
        scratch_shapes=state + state,
        compiler_params=_cparams(("arbitrary",)),
        name="mlstm",
    )(qc, kc, kct, z, gates, qc, kc, kct, z, gates, bias)


def _attn_prep_kernel(q_ref, k_ref, v_ref, qg_ref, kg_ref, invf_ref, qo_ref, ko_ref, vo_ref):
    tm = q_ref.shape[0]
    t = pl.program_id(0) * tm + lax.broadcasted_iota(jnp.int32, (tm, LANES), 0)
    lane = lax.broadcasted_iota(jnp.int32, (tm, LANES), 1)
    half = B_HEAD_DIM // 2
    grid_shift = GRID_W.bit_length() - 1
    pos = jnp.where(lane < half, t >> grid_shift, t & (GRID_W - 1)).astype(jnp.float32)
    ang = pos * invf_ref[...]
    cos = jnp.cos(ang)
    first = (lane & (half - 1)) < (half // 2)
    sin = jnp.where(first, -jnp.sin(ang), jnp.sin(ang))

    def norm_rope(x, g, scale):
        y = x * lax.rsqrt(jnp.mean(x * x, axis=-1, keepdims=True) + NORM_EPS) * g
        partner = jnp.where(first, pltpu.roll(y, LANES - half // 2, axis=1), pltpu.roll(y, half // 2, axis=1))
        return ((y * cos + partner * sin) * scale).astype(jnp.bfloat16)

    for hd in range(B_HEADS):
        sl = slice(hd * B_HEAD_DIM, (hd + 1) * B_HEAD_DIM)
        qo_ref[:, sl] = norm_rope(q_ref[:, sl], qg_ref[...], B_HEAD_DIM ** -0.5 * LOG2_E)
    ones_col = jnp.where(lane == 0, 1.0, 0.0).astype(jnp.bfloat16)
    for hd in range(B_KV_HEADS):
        sl = slice(hd * B_HEAD_DIM, (hd + 1) * B_HEAD_DIM)
        ko_ref[:, sl] = norm_rope(k_ref[:, sl], kg_ref[...], 1.0)
        vo_ref[:, 2 * hd * B_HEAD_DIM:(2 * hd + 1) * B_HEAD_DIM] = v_ref[:, sl].astype(jnp.bfloat16)
        vo_ref[:, (2 * hd + 1) * B_HEAD_DIM:(2 * hd + 2) * B_HEAD_DIM] = ones_col


def _attn_prep(z, q_norm_g, k_norm_g, inv_freq):
    tm = PREP_TM
    return pl.pallas_call(
        _attn_prep_kernel,
        out_shape=(jax.ShapeDtypeStruct((SEQ, B_WIDTH), jnp.bfloat16),
                   jax.ShapeDtypeStruct((SEQ, B_KV_WIDTH), jnp.bfloat16),
                   jax.ShapeDtypeStruct((SEQ, 2 * B_KV_WIDTH), jnp.bfloat16)),
        grid=(SEQ // tm,),
        in_specs=[pl.BlockSpec((tm, B_WIDTH), lambda i: (i, Z_QB // B_WIDTH)),
                  pl.BlockSpec((tm, B_KV_WIDTH), lambda i: (i, Z_KB // B_KV_WIDTH)),
                  pl.BlockSpec((tm, B_KV_WIDTH), lambda i: (i, Z_VB // B_KV_WIDTH)),
                  pl.BlockSpec((1, LANES), lambda i: (0, 0)),
                  pl.BlockSpec((1, LANES), lambda i: (0, 0)),
                  pl.BlockSpec((1, LANES), lambda i: (0, 0))],
        out_specs=(pl.BlockSpec((tm, B_WIDTH), lambda i: (i, 0)),
                   pl.BlockSpec((tm, B_KV_WIDTH), lambda i: (i, 0)),
                   pl.BlockSpec((tm, 2 * B_KV_WIDTH), lambda i: (i, 0))),
        compiler_params=_cparams(("arbitrary",)),
        name="attn_prep",
    )(z, z, z, q_norm_g, k_norm_g, inv_freq)


def _attn_kernel(q_ref, k_ref, v_ref, o_ref, s_ref, m_ref, acc_ref):
    group = B_HEADS // B_KV_HEADS
    tiles = ATT_TK // LANES
    q = jnp.concatenate([q_ref[:, g * B_HEAD_DIM:(g + 1) * B_HEAD_DIM] for g in range(group)], axis=0)
    m_ref[...] = jnp.full(m_ref.shape, -jnp.inf, jnp.float32)
    acc_ref[...] = jnp.zeros(acc_ref.shape, jnp.float32)

    def scores(c, carry):
        off = pl.multiple_of(c * ATT_TK, ATT_TK)
        s = _dot_nt(q, k_ref[pl.ds(off, ATT_TK), :])
        s_ref[:, pl.ds(off, ATT_TK)] = s
        part = s[:, :LANES]
        for j in range(1, tiles):
            part = jnp.maximum(part, s[:, j * LANES:(j + 1) * LANES])
        m_ref[...] = jnp.maximum(m_ref[...], part)
        return carry

    lax.fori_loop(0, SEQ // ATT_TK, scores, 0, unroll=8)
    row_max = jnp.max(m_ref[...], axis=1, keepdims=True)
    m_ref[...] = jnp.broadcast_to(row_max, m_ref.shape)

    def weighted(c, carry):
        off = pl.multiple_of(c * ATT_TK, ATT_TK)
        m = jnp.concatenate([m_ref[...]] * tiles, axis=1)
        p = jnp.exp2(s_ref[:, pl.ds(off, ATT_TK)] - m).astype(jnp.bfloat16)
        acc_ref[...] += _dot(p, v_ref[pl.ds(off, ATT_TK), :])
        return carry

    lax.fori_loop(0, SEQ // ATT_TK, weighted, 0, unroll=8)
    out = acc_ref[:, :B_HEAD_DIM] / acc_ref[:, B_HEAD_DIM:B_HEAD_DIM + 1]
    for g in range(group):
        o_ref[:, g * B_HEAD_DIM:(g + 1) * B_HEAD_DIM] = out[g * ATT_TQ:(g + 1) * ATT_TQ].astype(o_ref.dtype)


def _attn(qr, kr, vr):
    group = B_HEADS // B_KV_HEADS
    rows = group * ATT_TQ
    return pl.pallas_call(
        _attn_kernel,
        out_shape=jax.ShapeDtypeStruct((SEQ, B_WIDTH), jnp.bfloat16),
        grid=(B_KV_HEADS, SEQ // ATT_TQ),
        in_specs=[pl.BlockSpec((ATT_TQ, group * B_HEAD_DIM), lambda kv, i: (i, kv)),
                  pl.BlockSpec((SEQ, B_HEAD_DIM), lambda kv, i: (0, kv)),
                  pl.BlockSpec((SEQ, 2 * B_HEAD_DIM), lambda kv, i: (0, kv))],
        out_specs=pl.BlockSpec((ATT_TQ, group * B_HEAD_DIM), lambda kv, i: (i, kv)),
        scratch_shapes=[pltpu.VMEM((rows, SEQ), jnp.float32), pltpu.VMEM((rows, LANES), jnp.float32),
                        pltpu.VMEM((rows, 2 * B_HEAD_DIM), jnp.float32)],
        compiler_params=_cparams(("arbitrary", "arbitrary")),
        name="attn",
    )(qr, kr, vr)


def _mix_kernel(hf_ref, hb_ref, oa_ref, ng_ref, yb_ref, ga_ref, gb_ref, pa_ref, pb_ref, o_ref):
    parts = []
    for hd in range(A_HEADS):
        sl = slice(hd * A_HEAD_DIM, (hd + 1) * A_HEAD_DIM)
        hs = hf_ref[:, sl] + hb_ref[:, sl]
        y = hs * lax.rsqrt(jnp.mean(hs * hs, axis=-1, keepdims=True) + NORM_EPS) * ng_ref[:, sl]
        parts.append((_sigmoid(oa_ref[:, sl]) * y).astype(jnp.bfloat16))
    ya = jnp.concatenate(parts, axis=1)
    ta = _dot(ya, pa_ref[...])
    tb = _dot(yb_ref[...], pb_ref[...])
    o_ref[...] = (_sigmoid(ga_ref[...]) * ta + _sigmoid(gb_ref[...]) * tb).astype(o_ref.dtype)


def _mix(hf, hb, z, norm_g, yb, pa, pb):
    tm = MIX_TM
    return pl.pallas_call(
        _mix_kernel,
        out_shape=jax.ShapeDtypeStruct((SEQ, D_MODEL), jnp.bfloat16),
        grid=(SEQ // tm,),
        in_specs=[pl.BlockSpec((tm, A_WIDTH), lambda i: (i, 0)),
                  pl.BlockSpec((tm, A_WIDTH), lambda i: (i, 0)),
                  pl.BlockSpec((tm, A_WIDTH), lambda i: (i, Z_OA // A_WIDTH)),
                  pl.BlockSpec((1, A_WIDTH), lambda i: (0, 0)),
                  pl.BlockSpec((tm, B_WIDTH), lambda i: (i, 0)),
                  pl.BlockSpec((tm, D_MODEL), lambda i: (i, Z_GA // D_MODEL)),
                  pl.BlockSpec((tm, D_MODEL), lambda i: (i, Z_GB // D_MODEL)),
                  pl.BlockSpec((A_WIDTH, D_MODEL), lambda i: (0, 0)),
                  pl.BlockSpec((B_WIDTH, D_MODEL), lambda i: (0, 0))],
        out_specs=pl.BlockSpec((tm, D_MODEL), lambda i: (i, 0)),
        compiler_params=_cparams(("arbitrary",)),
        name="mix",
    )(hf, hb, z, norm_g, yb, z, z, pa, pb)


def _split_bf16(a):
    hi = a.astype(jnp.bfloat16)
    lo = (a - hi.astype(jnp.float32)).astype(jnp.bfloat16)
    return hi, lo


def _out_proj_kernel(mx_ref, w_ref, x_ref, g_ref, wr_ref, br_ref, h1_ref, xn_ref, lg_ref):
    h1 = x_ref[...] + _dot(mx_ref[...], w_ref[...])
    h1_ref[...] = h1
    xn = h1 * lax.rsqrt(jnp.mean(h1 * h1, axis=-1, keepdims=True) + NORM_EPS) * g_ref[...]
    xn_ref[...] = xn
    x_hi, x_lo = _split_bf16(xn)
    w_hi, w_lo = _split_bf16(wr_ref[...])
    logits = _dot(x_hi, w_hi) + (_dot(x_lo, w_hi) + _dot(x_hi, w_lo)) + br_ref[...]
    lg_ref[...] = logits.T[:N_EXPERTS, :]


def _out_proj(mixed, w_out, x, g, w_router, b_router):
    tm = OUT_TM
    return pl.pallas_call(
        _out_proj_kernel,
        out_shape=(jax.ShapeDtypeStruct((SEQ, D_MODEL), jnp.float32),
                   jax.ShapeDtypeStruct((SEQ, D_MODEL), jnp.float32),
                   jax.ShapeDtypeStruct((N_EXPERTS, SEQ), jnp.float32)),
        grid=(SEQ // tm,),
        in_specs=[pl.BlockSpec((tm, D_MODEL), lambda i: (i, 0)),
                  pl.BlockSpec((D_MODEL, D_MODEL), lambda i: (0, 0)),
                  pl.BlockSpec((tm, D_MODEL), lambda i: (i, 0)),
                  pl.BlockSpec((1, D_MODEL), lambda i: (0, 0)),
                  pl.BlockSpec((D_MODEL, LANES), lambda i: (0, 0)),
                  pl.BlockSpec((1, LANES), lambda i: (0, 0))],
        out_specs=(pl.BlockSpec((tm, D_MODEL), lambda i: (i, 0)),
                   pl.BlockSpec((tm, D_MODEL), lambda i: (i, 0)),
                   pl.BlockSpec((N_EXPERTS, tm), lambda i: (0, i))),
        compiler_params=_cparams(("arbitrary",)),
        name="out_proj",
    )(mixed, w_out, x, g, w_router, b_router)


def _route_kernel(lg_ref, e_ref, w_ref, pos_ref, cnt_ref, r_ref, carry_ref):
    tc = ROUTE_TC
    carry_ref[...] = jnp.zeros(carry_ref.shape, jnp.float32)
    eidx = lax.broadcasted_iota(jnp.int32, (N_EXPERTS, tc), 0)
    tri_r = lax.broadcasted_iota(jnp.int32, (tc, tc), 0)
    tri_c = lax.broadcasted_iota(jnp.int32, (tc, tc), 1)
    before = jnp.where(tri_r < tri_c, 1.0, 0.0).astype(jnp.bfloat16)

    def body(c, carry):
        off = pl.multiple_of(c * tc, tc)
        lg = lg_ref[:, pl.ds(off, tc)]
        vals, sels = [], []
        for _ in range(TOP_K):
            mx = jnp.max(lg, axis=0, keepdims=True)
            idx = jnp.min(jnp.where(lg == mx, eidx, N_EXPERTS), axis=0, keepdims=True)
            sel = eidx == idx
            vals.append(mx)
            sels.append(sel)
            lg = jnp.where(sel, -jnp.inf, lg)
        ex = [jnp.exp(v - vals[0]) for v in vals]
        tot = ex[0] + ex[1] + ex[2] + ex[3]
        chosen = jnp.where(sels[0] | sels[1] | sels[2] | sels[3], 1.0, 0.0)
        rank = _dot(chosen.astype(jnp.bfloat16), before) + carry_ref[:, 0:1]
        carry_ref[...] = carry_ref[...] + jnp.sum(chosen, axis=1, keepdims=True)
        for k in range(TOP_K):
            e_ref[k:k + 1, pl.ds(off, tc)] = jnp.sum(jnp.where(sels[k], eidx, 0), axis=0, keepdims=True)
            w_ref[k:k + 1, pl.ds(off, tc)] = ex[k] / tot
            r_ref[k:k + 1, pl.ds(off, tc)] = jnp.sum(jnp.where(sels[k], rank, 0.0), axis=0, keepdims=True)
        return carry

    lax.fori_loop(0, SEQ // tc, body, 0)

    counts = carry_ref[...].astype(jnp.int32)
    cnt_ref[...] = counts
    sub_shift = MOE_SUB.bit_length() - 1
    padded = (((counts + (MOE_SUB - 1)) >> sub_shift) << sub_shift).astype(jnp.float32)
    start = (_cumsum_rows(padded) - padded)[:, 0:1]

    def place(c, carry):
        off = pl.multiple_of(c * tc, tc)
        for k in range(TOP_K):
            sel = eidx == e_ref[k:k + 1, pl.ds(off, tc)]
            base = jnp.sum(jnp.where(sel, start, 0.0), axis=0, keepdims=True)
            pos_ref[k:k + 1, pl.ds(off, tc)] = (base + r_ref[k:k + 1, pl.ds(off, tc)]).astype(jnp.int32)
        return carry

    lax.fori_loop(0, SEQ // tc, place, 0)


def _route(logits_t):
    return pl.pallas_call(
        _route_kernel,
        out_shape=(jax.ShapeDtypeStruct((TOP_K, SEQ), jnp.int32),
                   jax.ShapeDtypeStruct((TOP_K, SEQ), jnp.float32),
                   jax.ShapeDtypeStruct((TOP_K, SEQ), jnp.int32),
                   jax.ShapeDtypeStruct((N_EXPERTS, LANES), jnp.int32)),
        scratch_shapes=[pltpu.VMEM((TOP_K, SEQ), jnp.float32), pltpu.VMEM((N_EXPERTS, LANES), jnp.float32)],
        compiler_params=_cparams(None),
        name="route",
    )(logits_t)


def _dispatch_kernel(pos_ref, xw_ref, xg_ref, sem):
    base = pl.program_id(0) * DISP_TT

    def copy(j, k):
        return pltpu.make_async_copy(xw_ref.at[pl.ds(j, 1)], xg_ref.at[pl.ds(pos_ref[k * SEQ + base + j], 1)], sem)

    def start(j, c):
        for k in range(TOP_K):
            copy(j, k).start()
        return c

    def wait(j, c):
        for k in range(TOP_K):
            copy(j, k).wait()
        return c

    lax.fori_loop(0, DISP_TT, start, 0, unroll=8)
    lax.fori_loop(0, DISP_TT, wait, 0, unroll=8)


def _dispatch(pos_flat, xw):
    return pl.pallas_call(
        _dispatch_kernel,
        out_shape=jax.ShapeDtypeStruct((MOE_ROWS, D_MODEL), jnp.float32),
        grid_spec=pltpu.PrefetchScalarGridSpec(
            num_scalar_prefetch=1,
            grid=(SEQ // DISP_TT,),
            in_specs=[pl.BlockSpec((DISP_TT, D_MODEL), lambda i, pos: (i, 0))],
            out_specs=pl.BlockSpec(memory_space=pl.ANY),
            scratch_shapes=[pltpu.SemaphoreType.DMA(())]),
        compiler_params=_cparams(("arbitrary",)),
        name="dispatch",
    )(pos_flat, xw)


def _experts_kernel(te_ref, sb_ref, ns_ref, nv_ref, na_ref, *refs):
    x_refs = refs[:MOE_NSUB]
    wg_ref, wu_ref, bg_ref, bu_ref, wd_ref, bd_ref, yg_ref, acc_ref, xs_ref, sem = refs[MOE_NSUB:]
    t = pl.program_id(0)
    f = pl.program_id(1)
    nf = pl.num_programs(1)

    def out_copy(first_group, j):
        rows = pl.ds(pl.multiple_of((first_group + j) * MOE_SUB, MOE_SUB), MOE_SUB)
        return pltpu.make_async_copy(acc_ref.at[pl.ds(j * MOE_SUB, MOE_SUB)], yg_ref.at[rows], sem)

    @pl.when(t < na_ref[0])
    def _():
        groups = ns_ref[t]
        valid = nv_ref[t]

        @pl.when(f == 0)
        def _():
            for j, x_ref in enumerate(x_refs):
                row = j * MOE_SUB + lax.broadcasted_iota(jnp.int32, x_ref.shape, 0)
                xs_ref[j * MOE_SUB:(j + 1) * MOE_SUB, :] = jnp.where(row < valid, x_ref[...], 0.0).astype(jnp.bfloat16)

        def ffn(n):
            rows = n * MOE_SUB
            x = xs_ref[:rows, :]
            gate = _dot(x, wg_ref[...].astype(jnp.bfloat16)) + bg_ref[...]
            up = _dot(x, wu_ref[...].astype(jnp.bfloat16)) + bu_ref[...]
            gate = jnp.minimum(gate, SWIGLU_LIMIT)
            up = jnp.clip(up, -SWIGLU_LIMIT, SWIGLU_LIMIT)
            act = ((up + 1.0) * (gate * _sigmoid(gate * SWIGLU_ALPHA))).astype(jnp.bfloat16)

            @pl.when(f == 0)
            def _():
                @pl.when(t > 0)
                def _():
                    for j in range(MOE_NSUB):
                        @pl.when(j < ns_ref[t - 1])
                        def _():
                            out_copy(sb_ref[t - 1], j).wait()

                acc_ref[:rows, :] = jnp.zeros((rows, D_MODEL), jnp.float32) + bd_ref[...]

            acc_ref[:rows, :] += _dot(act, wd_ref[...].astype(jnp.bfloat16))

            @pl.when(f == nf - 1)
            def _():
                for j in range(n):
                    out_copy(sb_ref[t], j).start()

                @pl.when(t == na_ref[0] - 1)
                def _():
                    for j in range(n):
                        out_copy(sb_ref[t], j).wait()

        for n in range(1, MOE_NSUB + 1):
            pl.when(groups == n)(functools.partial(ffn, n))


def _experts(tile_expert, tile_group0, tile_groups, tile_valid, n_active, xg, w_gate_up, b_gate_up, w_down, b_down):
    nf = D_FF // MOE_TF
    tf = MOE_TF
    last_group = MOE_ROWS // MOE_SUB - 1

    def tile(t, na):
        return jnp.minimum(t, na[0] - 1)

    def fcol(t, f, na):
        return jnp.where(t < na[0], f, nf - 1)

    def group(j):
        def index_map(t, f, te, sb, ns, nv, na):
            return (jnp.minimum(sb[tile(t, na)] + j, last_group), 0)
        return index_map

    def weight(row_block, col_block):
        def index_map(t, f, te, sb, ns, nv, na):
            fc = fcol(t, f, na)
            return (0, te[tile(t, na)], row_block(fc), col_block(fc))
        return index_map

    zero = lambda fc: 0
    same = lambda fc: fc
    upper = lambda fc: nf + fc
    return pl.pallas_call(
        _experts_kernel,
        out_shape=jax.ShapeDtypeStruct((MOE_ROWS, D_MODEL), jnp.float32),
        grid_spec=pltpu.PrefetchScalarGridSpec(
            num_scalar_prefetch=5,
            grid=(MOE_TILES, nf),
            in_specs=[pl.BlockSpec((MOE_SUB, D_MODEL), group(j)) for j in range(MOE_NSUB)] + [
                pl.BlockSpec((None, None, D_MODEL, tf), weight(zero, same)),
                pl.BlockSpec((None, None, D_MODEL, tf), weight(zero, upper)),
                pl.BlockSpec((None, None, 1, tf), weight(zero, same)),
                pl.BlockSpec((None, None, 1, tf), weight(zero, upper)),
                pl.BlockSpec((None, None, tf, D_MODEL), weight(same, zero)),
                pl.BlockSpec((None, None, 1, D_MODEL), weight(zero, zero)),
            ],
            out_specs=pl.BlockSpec(memory_space=pl.ANY),
            scratch_shapes=[pltpu.VMEM((MOE_TM, D_MODEL), jnp.float32), pltpu.VMEM((MOE_TM, D_MODEL), jnp.bfloat16),
                            pltpu.SemaphoreType.DMA(())]),
        compiler_params=_cparams(("arbitrary", "arbitrary"), vmem=EXPERTS_VMEM_LIMIT),
        name="experts",
    )(tile_expert, tile_group0, tile_groups, tile_valid, n_active, *([xg] * MOE_NSUB),
      w_gate_up, w_gate_up, b_gate_up, b_gate_up, w_down, b_down)


def _combine_kernel(pos_ref, yg_ref, w_ref, h1_ref, g_ref, o_ref, buf_ref, sem):
    tt = COMB_TT
    base = pl.program_id(0) * tt

    def copy(j, k):
        return pltpu.make_async_copy(yg_ref.at[pl.ds(pos_ref[k * SEQ + base + j], 1)],
                                     buf_ref.at[k, pl.ds(j, 1)], sem)

    def start(j, c):
        for k in range(TOP_K):
            copy(j, k).start()
        return c

    def wait(j, c):
        for k in range(TOP_K):
            copy(j, k).wait()
        return c

    lax.fori_loop(0, tt, start, 0, unroll=8)
    lax.fori_loop(0, tt, wait, 0, unroll=8)
    w = w_ref[...]
    h2 = h1_ref[...]
    for k in range(TOP_K):
        h2 = h2 + w[:, k:k + 1] * buf_ref[k]
    o_ref[...] = h2 * lax.rsqrt(jnp.mean(h2 * h2, axis=-1, keepdims=True) + NORM_EPS) * g_ref[...]


def _combine(pos_flat, yg, gate_w, h1, g):
    tt = COMB_TT
    return pl.pallas_call(
        _combine_kernel,
        out_shape=jax.ShapeDtypeStruct((SEQ, D_MODEL), jnp.float32),
        grid_spec=pltpu.PrefetchScalarGridSpec(
            num_scalar_prefetch=1,
            grid=(SEQ // tt,),
            in_specs=[pl.BlockSpec(memory_space=pl.ANY),
                      pl.BlockSpec((tt, TOP_K), lambda i, pos: (i, 0)),
                      pl.BlockSpec((tt, D_MODEL), lambda i, pos: (i, 0)),
                      pl.BlockSpec((1, D_MODEL), lambda i, pos: (0, 0))],
            out_specs=pl.BlockSpec((tt, D_MODEL), lambda i, pos: (i, 0)),
            scratch_shapes=[pltpu.VMEM((TOP_K, tt, D_MODEL), jnp.float32), pltpu.SemaphoreType.DMA(())]),
        compiler_params=_cparams(("arbitrary",)),
        name="combine",
    )(pos_flat, yg, gate_w, h1, g)


def _regroup_w_in(w):
    g0 = 4 * A_WIDTH
    b0 = g0 + 4 * A_HEADS
    m0 = b0 + B_WIDTH + 2 * B_KV_WIDTH
    main = jnp.concatenate([w[:, :g0], w[:, m0:], w[:, b0:m0]], axis=1).astype(jnp.bfloat16)
    gates = jnp.pad(w[:, g0:b0], ((0, 0), (0, LANES - 4 * A_HEADS))).astype(jnp.bfloat16)
    return main, gates


def _tile_metadata(counts):
    as_i32 = lambda a: a.astype(jnp.int32)
    groups = (counts + MOE_SUB - 1) // MOE_SUB
    group_end = jnp.cumsum(groups)
    tiles = (groups + MOE_NSUB - 1) // MOE_NSUB
    tile_end = jnp.cumsum(tiles)
    t = jnp.arange(MOE_TILES, dtype=jnp.int32)
    expert = jnp.minimum(jnp.sum(t[:, None] >= tile_end[None, :], axis=1), N_EXPERTS - 1)
    local = t - (tile_end - tiles)[expert]
    per_tile = groups[expert] // jnp.maximum(tiles[expert], 1)
    extra = groups[expert] - per_tile * tiles[expert]
    local_group0 = per_tile * local + jnp.minimum(local, extra)
    n_groups = jnp.where(local < tiles[expert], per_tile + (local < extra), 0)
    group0 = (group_end - groups)[expert] + local_group0
    valid = jnp.clip(counts[expert] - MOE_SUB * local_group0, 0, MOE_SUB * n_groups)
    return as_i32(expert), as_i32(group0), as_i32(n_groups), as_i32(valid), as_i32(tile_end[-1:])


def kernel(x, norm_mix_g, w_in, conv_w, conv_b, b_mlstm_gates, mlstm_norm_g, q_norm_g, k_norm_g, w_proj_a,
           w_proj_b, w_out, norm_ffn_g, w_router, b_router, w_gate_up, b_gate_up, w_down, b_down, norm_final_g):
    assert x.shape == (1, SEQ, D_MODEL) and w_in.shape[0] == 1
    x2 = x[0]
    w_main, w_gates = _regroup_w_in(w_in[0])
    z, gates = _in_proj(x2, norm_mix_g, w_main, w_gates)

    (qc,) = _conv(z, conv_w[0], conv_b, 0, A_HEAD_DIM ** -0.5, False)
    kc, kct = _conv(z, conv_w[0], conv_b, A_WIDTH // CONV_C, 1.0, True)
    gate_bias = jnp.pad(b_mlstm_gates.reshape(1, 4 * A_HEADS), ((0, 0), (0, LANES - 4 * A_HEADS)))
    hf, hb = _mlstm(qc, kc, kct, z, gates, gate_bias)

    half = B_HEAD_DIM // 2
    inv_freq = ROPE_THETA ** (-jnp.arange(0, half, 2, dtype=jnp.float32) / half)
    qr, kr, vr = _attn_prep(z, q_norm_g, k_norm_g, jnp.tile(inv_freq, 4).reshape(1, LANES))
    yb = _attn(qr, kr, vr)

    mixed = _mix(hf, hb, z, mlstm_norm_g, yb, w_proj_a[0].astype(jnp.bfloat16), w_proj_b[0].astype(jnp.bfloat16))
    lane_pad = ((0, 0), (0, LANES - N_EXPERTS))
    h1, xw, logits_t = _out_proj(mixed, w_out[0].astype(jnp.bfloat16), x2, norm_ffn_g,
                                 jnp.pad(w_router[0], lane_pad), jnp.pad(b_router, lane_pad))

    _, gate_w, pos, counts = _route(logits_t)
    pos = pos.reshape(-1)
    tile_meta = _tile_metadata(counts[:, 0])

    xg = _dispatch(pos, xw)
    yg = _experts(*tile_meta, xg, w_gate_up, b_gate_up.reshape(1, N_EXPERTS, 1, 2 * D_FF),
                  w_down, b_down.reshape(1, N_EXPERTS, 1, D_MODEL))
    out = _combine(pos, yg, gate_w.T, h1, norm_final_g.reshape(1, D_MODEL))
    return out[None]
```
